```python
import math
import jax, jax.numpy as jnp
from jax import lax
import numpy as np

D_MODEL = 1024
BATCH = 2
SEQ = 8192
DEPTH = 2
DEC_BATCH = 32
DEC_SEQ = 4
PAST_LEN = 8192
PAGE_SIZE = 128

N_META = 16
N_MIXERS = 2
N_ATTN_LAYERS = (DEPTH + 1) // 2
N_MLSTM_LAYERS = DEPTH // 2
ATT_HEADS = 8
ATT_DH = D_MODEL // (2 * ATT_HEADS)
ATT_QK = 2 * ATT_DH
ATT_VD = 2 * ATT_DH
ATT_SCALE = ATT_DH ** -0.5
ROT_DIM = ATT_DH // 4
ROPE_THETA = 500000.0
Q_BLOCK = 128
M_HEADS = 8
M_DQK = D_MODEL // (2 * M_HEADS)
M_DV = D_MODEL // M_HEADS
M_CHUNK = 64
GATE_SOFTCAP = 15.0
M_IN_W = 2 * M_HEADS * M_DQK + 2 * M_HEADS * M_DV + 2 * M_HEADS
N_EXPERTS = 16
N_GROUPS = 4
EXPERTS_PER_GROUP = N_EXPERTS // N_GROUPS
TOP_K = 2
D_EXPERT = D_MODEL // 2
DN_ALPHA = (2 * DEPTH) ** 0.25
DN_BETA = (8 * DEPTH) ** -0.25
LN_EPS = 1e-5
RMS_EPS = 1e-6

kernel_name = 'diffattn_mlstm_groupedmoe_deepnorm_step'

f32 = jnp.float32


def layer_norm(x, g, b):
    xf = x.astype(f32)
    mu = jnp.mean(xf, axis=-1, keepdims=True)
    xc = xf - mu
    var = jnp.mean(xc * xc, axis=-1, keepdims=True)
    return (xc * lax.rsqrt(var + LN_EPS) * g.astype(f32) + b.astype(f32)).astype(x.dtype)


def _partial_rope(x, pos):
    half = ROT_DIM // 2
    inv_freq = ROPE_THETA ** (-jnp.arange(half, dtype=f32) * (2.0 / ROT_DIM))
    ang = pos.astype(f32)[:, None] * inv_freq[None, :]
    cos = jnp.cos(ang)[:, None, :].astype(x.dtype)
    sin = jnp.sin(ang)[:, None, :].astype(x.dtype)
    x1 = x[..., :half]
    x2 = x[..., half:ROT_DIM]
    return jnp.concatenate([x1 * cos - x2 * sin, x2 * cos + x1 * sin, x[..., ROT_DIM:]], axis=-1)


def _diff_lambda(lq1, lk1, lq2, lk2, lam_init):
    return (jnp.exp(jnp.sum(lq1.astype(f32) * lk1.astype(f32)))
            - jnp.exp(jnp.sum(lq2.astype(f32) * lk2.astype(f32))) + lam_init)


def _diff_qkv(x, pos, w_in):
    B, T, _ = x.shape
    q, k, v = jnp.split(x @ w_in, 3, axis=-1)
    q = _partial_rope(q.reshape(B, T, 2 * ATT_HEADS, ATT_DH), pos).reshape(B, T, ATT_HEADS, 2, ATT_DH)
    k = _partial_rope(k.reshape(B, T, 2 * ATT_HEADS, ATT_DH), pos).reshape(B, T, ATT_HEADS, 2, ATT_DH)
    v = v.reshape(B, T, ATT_HEADS, ATT_VD)
    return q, k, v


def _diff_attend(q, k, v, q_idx, k_idx, lam):
    s = jnp.einsum('bqhcd,bkhcd->bhcqk', q, k).astype(f32) * ATT_SCALE
    mask = k_idx[None, :] <= q_idx[:, None]
    s = jnp.where(mask, s, -jnp.inf)
    p = jax.nn.softmax(s, axis=-1)
    a = p[:, :, 0] - lam * p[:, :, 1]
    return jnp.einsum('bhqk,bkhe->bqhe', a, v.astype(f32))


def _diff_out(o, lam_init, g, w_out, dtype):
    B, T = o.shape[:2]
    o = o * lax.rsqrt(jnp.mean(o * o, axis=-1, keepdims=True) + RMS_EPS) * g.astype(f32) * (1.0 - lam_init)
    return o.reshape(B, T, -1).astype(dtype) @ w_out


def diff_attention_prompt(x, w_in, lam, lam_init, g, w_out):
    B, L, _ = x.shape
    idx = jnp.arange(L, dtype=jnp.int32)
    q, k, v = _diff_qkv(x, idx, w_in)
    n_blk = -(-L // Q_BLOCK)
    pad = n_blk * Q_BLOCK - L
    padw = lambda a: jnp.pad(a, [(0, 0), (0, pad)] + [(0, 0)] * (a.ndim - 2))
    qp, kp, vp = padw(q), padw(k), padw(v)
    k_idx = jnp.arange(n_blk * Q_BLOCK, dtype=jnp.int32)
    q_blocks = jnp.moveaxis(qp.reshape(B, n_blk, Q_BLOCK, ATT_HEADS, 2, ATT_DH), 1, 0)
    idx_blocks = k_idx.reshape(n_blk, Q_BLOCK)
    o = lax.map(lambda qi: _diff_attend(qi[0], kp, vp, qi[1], k_idx, lam), (q_blocks, idx_blocks))
    o = jnp.moveaxis(o, 0, 1).reshape(B, n_blk * Q_BLOCK, ATT_HEADS, ATT_VD)[:, :L]
    y = _diff_out(o, lam_init, g, w_out, x.dtype)
    return y, k.reshape(B, L, ATT_HEADS, ATT_QK), v


def diff_attention_sample(x, cache_k, cache_v, page_table, w_in, lam, lam_init, g, w_out):
    Bd, T, _ = x.shape
    past = page_table.shape[1] * cache_k.shape[1]
    q_idx = past + jnp.arange(T, dtype=jnp.int32)
    q, k, v = _diff_qkv(x, q_idx, w_in)
    k_past = cache_k[page_table].reshape(Bd, past, ATT_HEADS, 2, ATT_DH).astype(k.dtype)
    v_past = cache_v[page_table].reshape(Bd, past, ATT_HEADS, ATT_VD).astype(v.dtype)
    k_all = jnp.concatenate([k_past, k], axis=1)
    v_all = jnp.concatenate([v_past, v], axis=1)
    k_idx = jnp.arange(past + T, dtype=jnp.int32)
    o = _diff_attend(q, k_all, v_all, q_idx, k_idx, lam)
    y = _diff_out(o, lam_init, g, w_out, x.dtype)
    return y, k.reshape(Bd, T, ATT_HEADS, ATT_QK), v


def _softcap(z):
    return GATE_SOFTCAP * jnp.tanh(z / GATE_SOFTCAP)


def _mlstm_proj(x, w_in, b_if):
    B, T, _ = x.shape
    z = (x @ w_in).astype(f32)
    qw = M_HEADS * M_DQK
    vw = M_HEADS * M_DV
    q, k, v, o, gi, gf = jnp.split(z, [qw, 2 * qw, 2 * qw + vw, 2 * qw + 2 * vw, 2 * qw + 2 * vw + M_HEADS], axis=-1)
    q = q.reshape(B, T, M_HEADS, M_DQK)
    k = k.reshape(B, T, M_HEADS, M_DQK) * (M_DQK ** -0.5)
    v = v.reshape(B, T, M_HEADS, M_DV)
    b = b_if.astype(f32)
    ig = _softcap(gi + b[0])
    lf = jax.nn.log_sigmoid(_softcap(gf + b[1]))
    return q, k, v, ig, lf, jax.nn.sigmoid(o)


def _mlstm_chunk(carry, inp):
    C0, n0, m0 = carry
    q, k, v, ig, lf = inp
    T = q.shape[1]
    b = jnp.cumsum(lf, axis=1)
    causal = jnp.tril(jnp.ones((T, T), dtype=bool))[None, :, :, None]
    logD = jnp.where(causal, b[:, :, None, :] - b[:, None, :, :] + ig[:, None, :, :], -jnp.inf)
    inter = b + m0[:, None, :]
    m = jnp.maximum(inter, jnp.max(logD, axis=2))
    w = jnp.einsum('bthd,bshd->btsh', q, k) * jnp.exp(logD - m[:, :, None, :])
    g_in = jnp.exp(inter - m)
    num = jnp.einsum('btsh,bshe->bthe', w, v) + g_in[..., None] * jnp.einsum('bhed,bthd->bthe', C0, q)
    den = jnp.sum(w, axis=2) + g_in * jnp.einsum('bhd,bthd->bth', n0, q)
    h = num / jnp.maximum(jnp.abs(den), jnp.exp(-m))[..., None]
    bL = b[:, -1]
    mL = m[:, -1]
    decay = jnp.exp(bL[:, None, :] - b + ig - mL[:, None, :])
    g0 = jnp.exp(bL + m0 - mL)
    C = g0[:, :, None, None] * C0 + jnp.einsum('bsh,bshe,bshd->bhed', decay, v, k)
    n = g0[..., None] * n0 + jnp.einsum('bsh,bshd->bhd', decay, k)
    return (C, n, mL), h


def _mlstm_out(h, o, g, w_out, dtype):
    B, T = h.shape[:2]
    h = h * lax.rsqrt(jnp.mean(h * h, axis=-1, keepdims=True) + RMS_EPS)
    h = h.reshape(B, T, -1) * g.astype(f32) * o
    return h.astype(dtype) @ w_out


def mlstm_prompt(x, w_in, b_if, g, w_out):
    B, L, _ = x.shape
    n_real = L - N_META
    nc = n_real // M_CHUNK
    q, k, v, ig, lf, o = _mlstm_proj(x, w_in, b_if)
    state0 = (jnp.zeros((B, M_HEADS, M_DV, M_DQK), f32), jnp.zeros((B, M_HEADS, M_DQK), f32),
              jnp.zeros((B, M_HEADS), f32))
    seqs = (q, k, v, ig, lf)
    state, h_meta = _mlstm_chunk(state0, tuple(a[:, :N_META] for a in seqs))

    def to_chunks(a):
        a = a[:, N_META:]
        return jnp.moveaxis(a.reshape((B, nc, M_CHUNK) + a.shape[2:]), 1, 0)

    state, h_ch = lax.scan(_mlstm_chunk, state, tuple(to_chunks(a) for a in seqs))
    h_rest = jnp.moveaxis(h_ch, 0, 1).reshape(B, n_real, M_HEADS, M_DV)
    h = jnp.concatenate([h_meta, h_rest], axis=1)
    return _mlstm_out(h, o, g, w_out, x.dtype), state


def mlstm_sample(x, C0, n0, m0, w_in, b_if, g, w_out):
    q, k, v, ig, lf, o = _mlstm_proj(x, w_in, b_if)
    state, h = _mlstm_chunk((C0.astype(f32), n0.astype(f32), m0.astype(f32)), (q, k, v, ig, lf))
    return _mlstm_out(h, o, g, w_out, x.dtype), state


def moe(x, w_router, b_router, wg, wu, wd):
    shp = x.shape
    xt = x.reshape(-1, shp[-1])
    n = xt.shape[0]
    s = jax.nn.sigmoid((xt @ w_router).astype(f32))
    sel = s + b_router.astype(f32)
    grp = sel.reshape(n, N_GROUPS, EXPERTS_PER_GROUP)
    grp_score = jnp.sum(lax.top_k(grp, TOP_K)[0], axis=-1)
    g_idx = jnp.argmax(grp_score, axis=-1)
    in_grp = grp[jnp.arange(n), g_idx]
    _, loc = lax.top_k(in_grp, TOP_K)
    e_idx = g_idx[:, None] * EXPERTS_PER_GROUP + loc
    s_sel = jnp.take_along_axis(s, e_idx, axis=-1)
    gates = s_sel / jnp.sum(s_sel, axis=-1, keepdims=True)
    dense_gate = jnp.sum(jax.nn.one_hot(e_idx, N_EXPERTS, dtype=f32) * gates[..., None], axis=1).astype(x.dtype)
    y = jnp.zeros_like(xt)
    for e in range(N_EXPERTS):
        h = jax.nn.silu(xt @ wg[e]) * (xt @ wu[e])
        y = y + dense_gate[:, e:e + 1] * (h @ wd[e])
    return y.reshape(shp)


def setup_inputs(seed: int = 0) -> dict:
    key = jax.random.key(seed)
    ks = jax.random.split(key, 32)
    nrm = lambda k, shape, scale: scale * jax.random.normal(k, shape, f32)
    n_pages = PAST_LEN // PAGE_SIZE
    n_used = DEC_BATCH * n_pages
    n_pool = n_used + max(1, n_used // 4)
    perm = jax.random.permutation(ks[0], n_pool)
    page_table = perm[:n_used].reshape(DEC_BATCH, n_pages).astype(jnp.int32)
    ig_bias = nrm(ks[1], (N_MLSTM_LAYERS, M_HEADS), 0.1)
    fg_bias = jnp.linspace(3.0, 6.0, M_HEADS, dtype=f32)[None, :] + nrm(ks[2], (N_MLSTM_LAYERS, M_HEADS), 0.1)
    return {
        'x_prompt': nrm(ks[3], (BATCH, SEQ, D_MODEL), 1.0),
        'x_sample': nrm(ks[4], (DEC_BATCH, DEC_SEQ, D_MODEL), 1.0),
        'cache_k': nrm(ks[5], (N_ATTN_LAYERS, n_pool, PAGE_SIZE, ATT_HEADS, ATT_QK), 1.0),
        'cache_v': nrm(ks[6], (N_ATTN_LAYERS, n_pool, PAGE_SIZE, ATT_HEADS, ATT_VD), 1.0),
        'state_C': nrm(ks[7], (N_MLSTM_LAYERS, DEC_BATCH, M_HEADS, M_DV, M_DQK), 0.3),
        'state_n': nrm(ks[8], (N_MLSTM_LAYERS, DEC_BATCH, M_HEADS, M_DQK), 0.3),
        'state_m': nrm(ks[9], (N_MLSTM_LAYERS, DEC_BATCH, M_HEADS), 1.0),
        'page_table': page_table,
        'meta_tokens': nrm(ks[10], (N_META, D_MODEL), 1.0),
        'w_attn_in': nrm(ks[11], (N_ATTN_LAYERS, D_MODEL, 3 * ATT_HEADS * ATT_QK), D_MODEL ** -0.5),
        'lambda_q1': nrm(ks[12], (N_ATTN_LAYERS, ATT_DH), 0.1),
        'lambda_k1': nrm(ks[13], (N_ATTN_LAYERS, ATT_DH), 0.1),
        'lambda_q2': nrm(ks[14], (N_ATTN_LAYERS, ATT_DH), 0.1),
        'lambda_k2': nrm(ks[15], (N_ATTN_LAYERS, ATT_DH), 0.1),
        'subln_g': 1.0 + nrm(ks[16], (N_ATTN_LAYERS, ATT_VD), 0.02),
        'w_attn_out': nrm(ks[17], (N_ATTN_LAYERS, ATT_HEADS * ATT_VD, D_MODEL), DN_BETA * (ATT_HEADS * ATT_VD) ** -0.5),
        'w_mlstm_in': nrm(ks[18], (N_MLSTM_LAYERS, D_MODEL, M_IN_W), D_MODEL ** -0.5),
        'b_mlstm_if': jnp.stack([ig_bias, fg_bias], axis=1),
        'mlstm_norm_g': 1.0 + nrm(ks[19], (N_MLSTM_LAYERS, M_HEADS * M_DV), 0.02),
        'w_mlstm_out': nrm(ks[20], (N_MLSTM_LAYERS, M_HEADS * M_DV, D_MODEL), DN_BETA * (M_HEADS * M_DV) ** -0.5),
        'w_router': nrm(ks[21], (D_MODEL, N_EXPERTS), D_MODEL ** -0.5),
        'b_router': nrm(ks[22], (N_EXPERTS,), 0.01),
        'w_exp_gate': nrm(ks[23], (DEPTH, N_EXPERTS, D_MODEL, D_EXPERT), D_MODEL ** -0.5),
        'w_exp_up': nrm(ks[24], (DEPTH, N_EXPERTS, D_MODEL, D_EXPERT), D_MODEL ** -0.5),
        'w_exp_down': nrm(ks[25], (DEPTH, N_EXPERTS, D_EXPERT, D_MODEL), DN_BETA * D_EXPERT ** -0.5),
        'ln_g': 1.0 + nrm(ks[26], (DEPTH, 2, D_MODEL), 0.02),
        'ln_b': nrm(ks[27], (DEPTH, 2, D_MODEL), 0.02),
    }


def reference(x_prompt, x_sample, cache_k, cache_v, state_C, state_n, state_m, page_table,
              meta_tokens, w_attn_in, lambda_q1, lambda_k1, lambda_q2, lambda_k2, subln_g,
              w_attn_out, w_mlstm_in, b_mlstm_if, mlstm_norm_g, w_mlstm_out, w_router,
              b_router, w_exp_gate, w_exp_up, w_exp_down, ln_g, ln_b):
    B = x_prompt.shape[0]
    meta = jnp.broadcast_to(meta_tokens.astype(x_prompt.dtype)[None], (B, N_META, D_MODEL))
    xp = jnp.concatenate([meta, x_prompt], axis=1)
    xs = x_sample
    kp_l, vp_l, ks_l, vs_l = [], [], [], []
    cp_l, np_l, mp_l, cs_l, ns_l, ms_l = [], [], [], [], [], []
    for i in range(DEPTH):
        if i % N_MIXERS == 0:
            a = i // N_MIXERS
            lam_init = 0.8 - 0.6 * math.exp(-0.3 * i)
            lam = _diff_lambda(lambda_q1[a], lambda_k1[a], lambda_q2[a], lambda_k2[a], lam_init)
            mix_p, k_p, v_p = diff_attention_prompt(xp, w_attn_in[a], lam, lam_init, subln_g[a], w_attn_out[a])
            mix_s, k_s, v_s = diff_attention_sample(xs, cache_k[a], cache_v[a], page_table, w_attn_in[a],
                                                    lam, lam_init, subln_g[a], w_attn_out[a])
            kp_l.append(k_p)
            vp_l.append(v_p)
            ks_l.append(k_s)
            vs_l.append(v_s)
        else:
            j = i // N_MIXERS
            mix_p, (c_p, n_p, m_p) = mlstm_prompt(xp, w_mlstm_in[j], b_mlstm_if[j], mlstm_norm_g[j], w_mlstm_out[j])
            mix_s, (c_s, n_s, m_s) = mlstm_sample(xs, state_C[j], state_n[j], state_m[j], w_mlstm_in[j],
                                                  b_mlstm_if[j], mlstm_norm_g[j], w_mlstm_out[j])
            cp_l.append(c_p)
            np_l.append(n_p)
            mp_l.append(m_p)
            cs_l.append(c_s)
            ns_l.append(n_s)
            ms_l.append(m_s)
        xp = layer_norm(DN_ALPHA * xp + mix_p.astype(xp.dtype), ln_g[i, 0], ln_b[i, 0])
        xs = layer_norm(DN_ALPHA * xs + mix_s.astype(xs.dtype), ln_g[i, 0], ln_b[i, 0])
        xp = layer_norm(DN_ALPHA * xp + moe(xp, w_router, b_router, w_exp_gate[i], w_exp_up[i], w_exp_down[i]),
                        ln_g[i, 1], ln_b[i, 1])
        xs = layer_norm(DN_ALPHA * xs + moe(xs, w_router, b_router, w_exp_gate[i], w_exp_up[i], w_exp_down[i]),
                        ln_g[i, 1], ln_b[i, 1])
    y_prompt = xp[:, N_META:]
    y_sample = xs
    k_prompt = jnp.stack(kp_l)
    v_prompt = jnp.stack(vp_l)
    k_sample = jnp.stack(ks_l)
    v_sample = jnp.stack(vs_l)
    C_prompt = jnp.stack(cp_l)
    n_prompt = jnp.stack(np_l)
    m_prompt = jnp.stack(mp_l)
    C_sample = jnp.stack(cs_l)
    n_sample = jnp.stack(ns_l)
    m_sample = jnp.stack(ms_l)
    return (y_prompt, y_sample, k_prompt, v_prompt, k_sample, v_sample,
            C_prompt, n_prompt, m_prompt, C_sample, n_sample, m_sample)
```

```python
import functools
import math

import jax
import jax.numpy as jnp
from jax import lax
from jax.experimental import pallas as pl
from jax.experimental.pallas import tpu as pltpu

f32 = jnp.float32
bf16 = jnp.bfloat16

D_MODEL = 1024
N_META = 16
HEADS = 8
HEAD_W = 128
SUB_W = 64
ATT_SCALE = SUB_W ** -0.5
ROT_DIM = 16
ROPE_THETA = 500000.0
M_DQK = 64
M_DV = 128
GATE_SOFTCAP = 15.0
N_EXPERTS = 16
N_GROUPS = 4
EXPERTS_PER_GROUP = 4
D_EXPERT = 512
DEPTH = 2
DN_ALPHA = (2 * DEPTH) ** 0.25
LN_EPS = 1e-5
RMS_EPS = 1e-6
NEG_BIG = -1e30

LANES = 128
MIB = 1024 * 1024

NT_DIMS = (((1,), (1,)), ((), ()))
TN_DIMS = (((0,), (0,)), ((), ()))


def _params(semantics, vmem_mib):
    return pltpu.CompilerParams(dimension_semantics=semantics, vmem_limit_bytes=vmem_mib * MIB)


def _split_hi_lo(a):
    hi = a.astype(bf16)
    lo = (a - hi.astype(f32)).astype(bf16)
    return hi, lo


def _layer_norm(r, g, b):
    mu = jnp.mean(r, axis=-1, keepdims=True)
    rc = r - mu
    var = jnp.mean(rc * rc, axis=-1, keepdims=True)
    return rc * lax.rsqrt(var + LN_EPS) * g + b


def _qkv_rope_kernel(x_ref, w_ref, c_ref, s1_ref, s2_ref,
                     qb_ref, kb_ref, vb_ref, kf_ref, vf_ref, *, seq_len, tm):
    i = pl.program_id(1)
    rows = i * tm + lax.broadcasted_iota(jnp.int32, (tm, 1), 0)
    valid = rows < seq_len
    x = x_ref[0]
    c = c_ref[...]
    s1 = s1_ref[...]
    s2 = s2_ref[...]
    for part in range(3):
        y = jnp.dot(x, w_ref[:, part * D_MODEL:(part + 1) * D_MODEL], preferred_element_type=f32)
        for h in range(HEADS):
            sl = slice(h * HEAD_W, (h + 1) * HEAD_W)
            yh = y[:, sl]
            if part < 2:
                yh = yh * c + pltpu.roll(yh, LANES - ROT_DIM // 2, 1) * s1 + pltpu.roll(yh, ROT_DIM // 2, 1) * s2
            if part == 0:
                qb_ref[0, :, sl] = jnp.where(valid, yh * ATT_SCALE, 0.0).astype(bf16)
            elif part == 1:
                kf_ref[0, :, sl] = yh
                kb_ref[0, :, sl] = jnp.where(valid, yh, 0.0).astype(bf16)
            else:
                vf_ref[0, :, sl] = yh
                vb_ref[0, :, sl] = jnp.where(valid, yh, 0.0).astype(bf16)


def _rope_tables(pos):
    half = ROT_DIM // 2
    inv_freq = ROPE_THETA ** (-jnp.arange(half, dtype=f32) * (2.0 / ROT_DIM))
    ang = pos.astype(f32)[:, None] * inv_freq[None, :]
    cos = jnp.cos(ang)
    sin = jnp.sin(ang)
    t = pos.shape[0]
    ones = jnp.ones((t, SUB_W - ROT_DIM), f32)
    zeros8 = jnp.zeros((t, half), f32)
    zeros48 = jnp.zeros((t, SUB_W - ROT_DIM), f32)
    c = jnp.concatenate([cos, cos, ones], axis=1)
    s1 = jnp.concatenate([-sin, zeros8, zeros48], axis=1)
    s2 = jnp.concatenate([zeros8, sin, zeros48], axis=1)
    return tuple(jnp.concatenate([a, a], axis=1) for a in (c, s1, s2))


def _qkv_rope(x_b16, w_b16, pos, tm):
    nb, seq_len, _ = x_b16.shape
    nt = pl.cdiv(seq_len, tm)
    lp = nt * tm
    c, s1, s2 = _rope_tables(pos)
    row_spec = pl.BlockSpec((1, tm, D_MODEL), lambda b, i: (b, i, 0))
    tab_spec = pl.BlockSpec((tm, HEAD_W), lambda b, i: (i, 0))
    pad_shape = jax.ShapeDtypeStruct((nb, lp, D_MODEL), bf16)
    out_shape = jax.ShapeDtypeStruct((nb, seq_len, D_MODEL), f32)
    return pl.pallas_call(
        functools.partial(_qkv_rope_kernel, seq_len=seq_len, tm=tm),
        out_shape=(pad_shape, pad_shape, pad_shape, out_shape, out_shape),
        grid=(nb, nt),
        in_specs=[row_spec, pl.BlockSpec((D_MODEL, 3 * D_MODEL), lambda b, i: (0, 0)), tab_spec, tab_spec, tab_spec],
        out_specs=(row_spec, row_spec, row_spec, row_spec, row_spec),
        compiler_params=_params(("parallel", "parallel"), 48),
        name="qkv_rope",
    )(x_b16, w_b16, c, s1, s2)


def _diff_lambda(lam_ref, lam_init):
    p = lam_ref[...]
    l1 = jnp.sum(p[0:1] * p[1:2], axis=-1, keepdims=True)
    l2 = jnp.sum(p[2:3] * p[3:4], axis=-1, keepdims=True)
    return jnp.exp(l1) - jnp.exp(l2) + lam_init


def _diff_finish(o1, o2, lam, g, lam_init):
    o = o1 - lam * o2
    o = o * lax.rsqrt(jnp.mean(o * o, axis=-1, keepdims=True) + RMS_EPS)
    return o * g * (1.0 - lam_init)


def _flash_kernel(lam_ref, g_ref, q_ref, k_ref, v_ref, o_ref, qq_sc, m_sc, l_sc, acc_sc, *, lam_init, tq):
    qi = pl.program_id(2)
    q = q_ref[0]
    lane = lax.broadcasted_iota(jnp.int32, (tq, HEAD_W), 1)
    zero = jnp.zeros_like(q)
    qq_sc[0:tq, :] = jnp.where(lane < SUB_W, q, zero)
    qq_sc[tq:2 * tq, :] = jnp.where(lane >= SUB_W, q, zero)
    m_sc[...] = jnp.full(m_sc.shape, NEG_BIG, f32)
    l_sc[...] = jnp.zeros(l_sc.shape, f32)
    acc_sc[...] = jnp.zeros(acc_sc.shape, f32)

    def step(kstart, masked):
        k = k_ref[0, pl.ds(kstart, tq), :]
        v = v_ref[0, pl.ds(kstart, tq), :]
        s = lax.dot_general(qq_sc[...], k, NT_DIMS, preferred_element_type=f32)
        if masked:
            r = lax.broadcasted_iota(jnp.int32, (2 * tq, tq), 0)
            r = jnp.where(r >= tq, r - tq, r)
            cidx = lax.broadcasted_iota(jnp.int32, (2 * tq, tq), 1)
            s = jnp.where(cidx <= r, s, NEG_BIG)
        m_prev = m_sc[...]
        m_new = jnp.maximum(m_prev, jnp.max(s, axis=-1, keepdims=True))
        p = jnp.exp(s - m_new)
        alpha = jnp.exp(m_prev - m_new)
        l_sc[...] = alpha * l_sc[...] + jnp.sum(p, axis=-1, keepdims=True)
        acc_sc[...] = alpha * acc_sc[...] + jnp.dot(p.astype(bf16), v, preferred_element_type=f32)
        m_sc[...] = m_new

    def body(ki, carry):
        step(pl.multiple_of(ki * tq, tq), False)
        return carry

    lax.fori_loop(0, qi, body, 0)
    step(pl.multiple_of(qi * tq, tq), True)

    lam = _diff_lambda(lam_ref, lam_init)
    o1 = acc_sc[0:tq, :] / l_sc[0:tq, :]
    o2 = acc_sc[tq:2 * tq, :] / l_sc[tq:2 * tq, :]
    o_ref[0] = _diff_finish(o1, o2, lam, g_ref[...], lam_init).astype(bf16)


def _flash_attention(qb, kb, vb, lam_p, g, lam_init, tq):
    nb, lp, _ = qb.shape
    nq = lp // tq
    q_spec = pl.BlockSpec((1, tq, HEAD_W), lambda b, h, i: (b, i, h))
    kv_spec = pl.BlockSpec((1, lp, HEAD_W), lambda b, h, i: (b, 0, h))
    return pl.pallas_call(
        functools.partial(_flash_kernel, lam_init=lam_init, tq=tq),
        out_shape=jax.ShapeDtypeStruct((nb, lp, D_MODEL), bf16),
        grid=(nb, HEADS, nq),
        in_specs=[pl.BlockSpec((4, SUB_W), lambda b, h, i: (0, 0)),
                  pl.BlockSpec((1, HEAD_W), lambda b, h, i: (0, 0)),
                  q_spec, kv_spec, kv_spec],
        out_specs=q_spec,
        scratch_shapes=[pltpu.VMEM((2 * tq, HEAD_W), bf16), pltpu.VMEM((2 * tq, 1), f32),
                        pltpu.VMEM((2 * tq, 1), f32), pltpu.VMEM((2 * tq, HEAD_W), f32)],
        compiler_params=_params(("parallel", "parallel", "arbitrary"), 40),
        name="flash_diff_attention",
    )(lam_p, g, qb, kb, vb)


def _decode_kernel(pt_ref, lam_ref, g_ref, q_ref, kn_ref, vn_ref, *rest, lam_init, pages_per_step, n_new):
    k_refs = rest[:pages_per_step]
    v_refs = rest[pages_per_step:2 * pages_per_step]
    o_ref = rest[2 * pages_per_step]
    m_sc, l_sc, acc_sc = rest[2 * pages_per_step + 1:]
    del pt_ref
    p_idx = pl.program_id(1)
    n_rows = HEADS * 2 * n_new
    page = k_refs[0].shape[1]

    @pl.when(p_idx == 0)
    def _():
        m_sc[...] = jnp.full(m_sc.shape, NEG_BIG, f32)
        l_sc[...] = jnp.zeros(l_sc.shape, f32)
        acc_sc[...] = jnp.zeros(acc_sc.shape, f32)

    q = q_ref[0]

    def step(k_ref, v_ref, masked):
        k = k_ref[0].astype(bf16)
        v = v_ref[0].astype(bf16)
        s = lax.dot_general(q, k, NT_DIMS, preferred_element_type=f32)
        if masked:
            r = lax.broadcasted_iota(jnp.int32, (n_rows, page), 0)
            tok = r % n_new
            cidx = lax.broadcasted_iota(jnp.int32, (n_rows, page), 1)
            s = jnp.where(cidx <= tok, s, NEG_BIG)
        m_prev = m_sc[...]
        m_new = jnp.maximum(m_prev, jnp.max(s, axis=-1, keepdims=True))
        p = jnp.exp(s - m_new)
        alpha = jnp.exp(m_prev - m_new)
        l_sc[...] = alpha * l_sc[...] + jnp.sum(p, axis=-1, keepdims=True)
        pv = jnp.dot(p.astype(bf16), v, preferred_element_type=f32)
        rph = 2 * n_new
        for h in range(HEADS):
            rs = slice(h * rph, (h + 1) * rph)
            acc_sc[rs, :] = alpha[rs, :] * acc_sc[rs, :] + pv[rs, h * HEAD_W:(h + 1) * HEAD_W]
        m_sc[...] = m_new

    for j in range(pages_per_step):
        step(k_refs[j], v_refs[j], False)

    @pl.when(p_idx == pl.num_programs(1) - 1)
    def _():
        step(kn_ref, vn_ref, True)
        lam = _diff_lambda(lam_ref, lam_init)
        o = acc_sc[...] / l_sc[...]
        rph = 2 * n_new
        for h in range(HEADS):
            o1 = o[h * rph:h * rph + n_new, :]
            o2 = o[h * rph + n_new:(h + 1) * rph, :]
            o_ref[0, :, h * HEAD_W:(h + 1) * HEAD_W] = _diff_finish(o1, o2, lam, g_ref[...], lam_init)


def _decode_attention(q_b16, k_new, v_new, cache_k, cache_v, page_table, lam_p, g, lam_init, pages_per_step):
    nbd, n_new, _ = q_b16.shape
    page = cache_k.shape[1]
    n_pages = page_table.shape[1]
    n_rows = HEADS * 2 * n_new
    row_sub = jnp.arange(n_rows, dtype=jnp.int32) // n_new
    col_sub = jnp.arange(D_MODEL, dtype=jnp.int32) // SUB_W
    q_rep = jnp.tile(q_b16[:, None], (1, 2 * HEADS, 1, 1)).reshape(nbd, n_rows, D_MODEL)
    q_bd = jnp.where(row_sub[None, :, None] == col_sub[None, None, :], q_rep, jnp.zeros_like(q_rep))
    pad = [(0, 0), (0, page - n_new), (0, 0)]
    kn = jnp.pad(k_new, pad)
    vn = jnp.pad(v_new, pad)
    steps = n_pages // pages_per_step

    def page_spec(j):
        return pl.BlockSpec((1, page, D_MODEL), lambda b, p, pt: (pt[b, p * pages_per_step + j], 0, 0))

    per_seq = lambda rows: pl.BlockSpec((1, rows, D_MODEL), lambda b, p, pt: (b, 0, 0))
    grid_spec = pltpu.PrefetchScalarGridSpec(
        num_scalar_prefetch=1,
        grid=(nbd, steps),
        in_specs=[pl.BlockSpec((4, SUB_W), lambda b, p, pt: (0, 0)),
                  pl.BlockSpec((1, HEAD_W), lambda b, p, pt: (0, 0)),
                  per_seq(n_rows), per_seq(page), per_seq(page)]
                 + [page_spec(j) for j in range(pages_per_step)]
                 + [page_spec(j) for j in range(pages_per_step)],
        out_specs=per_seq(n_new),
        scratch_shapes=[pltpu.VMEM((n_rows, 1), f32), pltpu.VMEM((n_rows, 1), f32),
                        pltpu.VMEM((n_rows, HEAD_W), f32)],
    )
    return pl.pallas_call(
        functools.partial(_decode_kernel, lam_init=lam_init, pages_per_step=pages_per_step, n_new=n_new),
        out_shape=jax.ShapeDtypeStruct((nbd, n_new, D_MODEL), f32),
        grid_spec=grid_spec,
        compiler_params=_params(("parallel", "arbitrary"), 40),
        name="paged_decode_attention",
    )(page_table, lam_p, g, q_bd, kn, vn, *([cache_k] * pages_per_step), *([cache_v] * pages_per_step))


def _proj_ln_kernel(o_ref, x_ref, w_ref, g_ref, b_ref, xo_ref, xb_ref):
    y = jnp.dot(o_ref[0], w_ref[...], preferred_element_type=f32)
    out = _layer_norm(DN_ALPHA * x_ref[0] + y, g_ref[...], b_ref[...])
    xo_ref[0] = out
    xb_ref[0] = out.astype(bf16)


def _proj_ln(o_b16, x, w_b16, ln_g, ln_b, tm):
    nb, seq_len, _ = x.shape
    row_spec = pl.BlockSpec((1, tm, D_MODEL), lambda b, i: (b, i, 0))
    vec_spec = pl.BlockSpec((1, D_MODEL), lambda b, i: (0, 0))
    return pl.pallas_call(
        _proj_ln_kernel,
        out_shape=(jax.ShapeDtypeStruct(x.shape, f32), jax.ShapeDtypeStruct(x.shape, bf16)),
        grid=(nb, pl.cdiv(seq_len, tm)),
        in_specs=[row_spec, row_spec, pl.BlockSpec((D_MODEL, D_MODEL), lambda b, i: (0, 0)), vec_spec, vec_spec],
        out_specs=(row_spec, row_spec),
        compiler_params=_params(("parallel", "parallel"), 32),
        name="proj_residual_layernorm",
    )(o_b16, x, w_b16, ln_g, ln_b)


def _router_kernel(x_ref, wh_ref, wl_ref, b_ref, gate_ref):
    x = x_ref[...]
    logits = (lax.dot_general(wh_ref[...], x, NT_DIMS, preferred_element_type=f32)
              + lax.dot_general(wl_ref[...], x, NT_DIMS, preferred_element_type=f32))
    s = jax.nn.sigmoid(logits)
    sel = s + b_ref[...]
    rows = [sel[e:e + 1, :] for e in range(N_EXPERTS)]
    scores = []
    for gidx in range(N_GROUPS):
        r = rows[gidx * EXPERTS_PER_GROUP:(gidx + 1) * EXPERTS_PER_GROUP]
        best = None
        for a in range(EXPERTS_PER_GROUP):
            for b in range(a + 1, EXPERTS_PER_GROUP):
                pair = r[a] + r[b]
                best = pair if best is None else jnp.maximum(best, pair)
        scores.append(best)
    picked = []
    for e in range(N_EXPERTS):
        gidx, loc = divmod(e, EXPERTS_PER_GROUP)
        win = None
        for j in range(N_GROUPS):
            if j == gidx:
                continue
            c = scores[gidx] > scores[j] if j < gidx else scores[gidx] >= scores[j]
            win = c if win is None else jnp.logical_and(win, c)
        rank = jnp.zeros_like(rows[e])
        for j in range(EXPERTS_PER_GROUP):
            if j == loc:
                continue
            o = rows[gidx * EXPERTS_PER_GROUP + j]
            ahead = o >= rows[e] if j < loc else o > rows[e]
            rank = rank + ahead.astype(f32)
        chosen = jnp.logical_and(win, rank < 1.5)
        picked.append(jnp.where(chosen, s[e:e + 1, :], 0.0))
    total = picked[0]
    for e in range(1, N_EXPERTS):
        total = total + picked[e]
    for e in range(N_EXPERTS):
        gate_ref[e:e + 1, :] = picked[e] / total


def _router(x_b16, w_hi, w_lo, b_col, tm):
    n_rows = x_b16.shape[0]
    return pl.pallas_call(
        _router_kernel,
        out_shape=jax.ShapeDtypeStruct((N_EXPERTS, n_rows), f32),
        grid=(pl.cdiv(n_rows, tm),),
        in_specs=[pl.BlockSpec((tm, D_MODEL), lambda i: (i, 0)),
                  pl.BlockSpec((N_EXPERTS, D_MODEL), lambda i: (0, 0)),
                  pl.BlockSpec((N_EXPERTS, D_MODEL), lambda i: (0, 0)),
                  pl.BlockSpec((N_EXPERTS, 1), lambda i: (0, 0))],
        out_specs=pl.BlockSpec((N_EXPERTS, tm), lambda i: (0, i)),
        compiler_params=_params(("parallel",), 32),
        name="moe_router",
    )(x_b16, w_hi, w_lo, b_col)


def _moe_kernel(xb_ref, x_ref, gate_ref, wg_ref, wu_ref, wd_ref, g_ref, b_ref, xo_ref, xbo_ref, acc_sc):
    e = pl.program_id(1)

    @pl.when(e == 0)
    def _():
        acc_sc[...] = jnp.zeros(acc_sc.shape, f32)

    x = xb_ref[...]
    a = jnp.dot(x, wg_ref[0].astype(bf16), preferred_element_type=f32)
    u = jnp.dot(x, wu_ref[0].astype(bf16), preferred_element_type=f32)
    hmid = (a * jax.nn.sigmoid(a) * u).astype(bf16)
    y = jnp.dot(hmid, wd_ref[0].astype(bf16), preferred_element_type=f32)
    gate = gate_ref[...]
    lane = lax.broadcasted_iota(jnp.int32, gate.shape, 1)
    gcol = jnp.sum(jnp.where(lane == e, gate, 0.0), axis=-1, keepdims=True)
    acc_sc[...] += gcol * y

    @pl.when(e == N_EXPERTS - 1)
    def _():
        out = _layer_norm(DN_ALPHA * x_ref[...] + acc_sc[...], g_ref[...], b_ref[...])
        xo_ref[...] = out
        xbo_ref[...] = out.astype(bf16)


def _moe_ln(x, x_b16, gate, wg, wu, wd, ln_g, ln_b, tm):
    n_rows = x.shape[0]
    row_spec = pl.BlockSpec((tm, D_MODEL), lambda i, e: (i, 0))
    vec_spec = pl.BlockSpec((1, D_MODEL), lambda i, e: (0, 0))
    return pl.pallas_call(
        _moe_kernel,
        out_shape=(jax.ShapeDtypeStruct(x.shape, f32), jax.ShapeDtypeStruct(x.shape, bf16)),
        grid=(pl.cdiv(n_rows, tm), N_EXPERTS),
        in_specs=[row_spec, row_spec, pl.BlockSpec((tm, N_EXPERTS), lambda i, e: (i, 0)),
                  pl.BlockSpec((1, D_MODEL, D_EXPERT), lambda i, e: (e, 0, 0)),
                  pl.BlockSpec((1, D_MODEL, D_EXPERT), lambda i, e: (e, 0, 0)),
                  pl.BlockSpec((1, D_EXPERT, D_MODEL), lambda i, e: (e, 0, 0)),
                  vec_spec, vec_spec],
        out_specs=(row_spec, row_spec),
        scratch_shapes=[pltpu.VMEM((tm, D_MODEL), f32)],
        compiler_params=_params(("parallel", "arbitrary"), 56),
        name="moe_experts_layernorm",
    )(x_b16, x, gate, wg, wu, wd, ln_g, ln_b)


def _moe_block(x, x_b16, router_w, wg, wu, wd, ln_g, ln_b, tm_router, tm_moe):
    shp = x.shape
    xf = x.reshape(-1, D_MODEL)
    xbf = x_b16.reshape(-1, D_MODEL)
    gate_t = _router(xbf, *router_w, tm_router)
    xo, xbo = _moe_ln(xf, xbf, gate_t.T, wg, wu, wd, ln_g, ln_b, tm_moe)
    return xo.reshape(shp), xbo.reshape(shp)


def _mlstm_proj_kernel(x_ref, w_ref, wgh_ref, wgl_ref, wgth_ref, wgtl_ref, z_ref, gcol_ref, grow_ref, *, seq_len, tm):
    i = pl.program_id(1)
    rows = i * tm + lax.broadcasted_iota(jnp.int32, (tm, 1), 0)
    x = x_ref[0]
    x = jnp.where(rows < seq_len, x, jnp.zeros_like(x))
    for part in range(3):
        sl = slice(part * D_MODEL, (part + 1) * D_MODEL)
        z_ref[0, :, sl] = jnp.dot(x, w_ref[:, sl], preferred_element_type=f32).astype(bf16)
    gcol_ref[0] = (jnp.dot(x, wgh_ref[...], preferred_element_type=f32)
                   + jnp.dot(x, wgl_ref[...], preferred_element_type=f32))
    grow_ref[0] = (lax.dot_general(wgth_ref[...], x, NT_DIMS, preferred_element_type=f32)
                   + lax.dot_general(wgtl_ref[...], x, NT_DIMS, preferred_element_type=f32))


def _mlstm_proj(x_b16, w_main, wg_cols, wg_rows, tm):
    nb, seq_len, _ = x_b16.shape
    nt = pl.cdiv(seq_len, tm)
    lp = nt * tm
    const = lambda shape: pl.BlockSpec(shape, lambda b, i: (0,) * len(shape))
    return pl.pallas_call(
        functools.partial(_mlstm_proj_kernel, seq_len=seq_len, tm=tm),
        out_shape=(jax.ShapeDtypeStruct((nb, lp, 3 * D_MODEL), bf16),
                   jax.ShapeDtypeStruct((nb, lp, LANES), f32),
                   jax.ShapeDtypeStruct((nb, 2 * HEADS, lp), f32)),
        grid=(nb, nt),
        in_specs=[pl.BlockSpec((1, tm, D_MODEL), lambda b, i: (b, i, 0)),
                  const((D_MODEL, 3 * D_MODEL)), const((D_MODEL, LANES)), const((D_MODEL, LANES)),
                  const((2 * HEADS, D_MODEL)), const((2 * HEADS, D_MODEL))],
        out_specs=(pl.BlockSpec((1, tm, 3 * D_MODEL), lambda b, i: (b, i, 0)),
                   pl.BlockSpec((1, tm, LANES), lambda b, i: (b, i, 0)),
                   pl.BlockSpec((1, 2 * HEADS, tm), lambda b, i: (b, 0, i))),
        compiler_params=_params(("parallel", "parallel"), 48),
        name="mlstm_in_proj",
    )(x_b16, w_main, *wg_cols, *wg_rows)


def _softcap(z):
    return GATE_SOFTCAP * jnp.tanh(z / GATE_SOFTCAP)


def _log_sigmoid(z):
    return jnp.minimum(z, 0.0) - jnp.log(1.0 + jnp.exp(-jnp.abs(z)))


def _mlstm_chunk_kernel(z_ref, gcol_ref, grow_ref, bcol_ref, brow_ref, g_ref, c0_ref, n0_ref, m0_ref,
                        h_ref, c_ref, n_ref, m_ref, c_sc, n_sc, m_sc, *, seq_len, tc):
    ci = pl.program_id(1)

    @pl.when(ci == 0)
    def _():
        c_sc[...] = c0_ref[0]
        n_sc[...] = n0_ref[0]
        m_sc[...] = m0_ref[0]

    t_col = ci * tc + lax.broadcasted_iota(jnp.int32, (tc, LANES), 0)
    lane = lax.broadcasted_iota(jnp.int32, (tc, LANES), 1)
    pre = _softcap(gcol_ref[0] + bcol_ref[...])
    gates_c = jnp.where(lane < HEADS, pre, _log_sigmoid(pre))
    neutral_c = jnp.where(lane < HEADS, NEG_BIG, 0.0)
    gates_c = jnp.where(t_col < seq_len, gates_c, neutral_c)
    t_row = ci * tc + lax.broadcasted_iota(jnp.int32, (2 * HEADS, tc), 1)
    sub = lax.broadcasted_iota(jnp.int32, (2 * HEADS, tc), 0)
    pre_r = _softcap(grow_ref[0] + brow_ref[...])
    gates_r = jnp.where(sub < HEADS, pre_r, _log_sigmoid(pre_r))
    neutral_r = jnp.where(sub < HEADS, NEG_BIG, 0.0)
    gates_r = jnp.where(t_row < seq_len, gates_r, neutral_r)

    ri = lax.broadcasted_iota(jnp.int32, (tc, tc), 0)
    cj = lax.broadcasted_iota(jnp.int32, (tc, tc), 1)
    causal = cj <= ri
    tril = jnp.where(causal, 1.0, 0.0).astype(bf16)
    triu = jnp.where(ri <= cj, 1.0, 0.0).astype(bf16)
    hi, lo = _split_hi_lo(gates_c)
    cum_c = jnp.dot(tril, hi, preferred_element_type=f32) + jnp.dot(tril, lo, preferred_element_type=f32)
    hi, lo = _split_hi_lo(gates_r)
    cum_r = jnp.dot(hi, triu, preferred_element_type=f32) + jnp.dot(lo, triu, preferred_element_type=f32)

    k_scale = M_DQK ** -0.5
    for h in range(HEADS):
        q = z_ref[0, :, h * M_DQK:(h + 1) * M_DQK]
        k = z_ref[0, :, HEADS * M_DQK + h * M_DQK:HEADS * M_DQK + (h + 1) * M_DQK]
        v = z_ref[0, :, D_MODEL + h * M_DV:D_MODEL + (h + 1) * M_DV]
        og = z_ref[0, :, 2 * D_MODEL + h * M_DV:2 * D_MODEL + (h + 1) * M_DV]
        b_col = cum_c[:, HEADS + h:HEADS + h + 1]
        a_col = gates_c[:, h:h + 1] - b_col
        a_row = gates_r[h:h + 1, :] - cum_r[HEADS + h:HEADS + h + 1, :]
        m0 = m_sc[h:h + 1, 0:1]
        c0 = c_sc[h]
        n0 = n_sc[h:h + 1, :]

        amat = jnp.where(causal, a_row, NEG_BIG)
        u = jnp.maximum(m0, jnp.max(amat, axis=-1, keepdims=True))
        dmat = jnp.exp(amat - u)
        s = lax.dot_general(q, k, NT_DIMS, preferred_element_type=f32) * k_scale
        w = s * dmat
        g_in = jnp.exp(m0 - u)
        num = (jnp.dot(w.astype(bf16), v, preferred_element_type=f32)
               + g_in * jnp.dot(q, c0.astype(bf16), preferred_element_type=f32))
        qn = jnp.sum(q.astype(f32) * n0, axis=-1, keepdims=True)
        den = jnp.sum(w, axis=-1, keepdims=True) + g_in * qn
        m_tok = b_col + u
        hval = num / jnp.maximum(jnp.abs(den), jnp.exp(-m_tok))
        hval = hval * lax.rsqrt(jnp.mean(hval * hval, axis=-1, keepdims=True) + RMS_EPS)
        hval = hval * g_ref[:, h * M_DV:(h + 1) * M_DV] * jax.nn.sigmoid(og.astype(f32))
        h_ref[0, :, h * M_DV:(h + 1) * M_DV] = hval.astype(bf16)

        u_last = u[tc - 1:tc, :]
        g0 = jnp.exp(m0 - u_last)
        decay = jnp.exp(a_col - u_last)
        kd = k.astype(f32) * (decay * k_scale)
        c_sc[h] = g0 * c0 + lax.dot_general(kd.astype(bf16), v, TN_DIMS, preferred_element_type=f32)
        n_sc[h:h + 1, :] = g0 * n0 + jnp.sum(kd, axis=0, keepdims=True)
        m_sc[h:h + 1, :] = jnp.broadcast_to(b_col[tc - 1:tc, :] + u_last, (1, LANES))

    @pl.when(ci == pl.num_programs(1) - 1)
    def _():
        c_ref[0] = c_sc[...]
        n_ref[0] = n_sc[...]
        m_ref[0] = m_sc[...]


def _mlstm_chunks(z, gcol, grow, bcol, brow, norm_g, c0t, n0, m0b, seq_len, tc):
    nb, lp, _ = z.shape
    nc = lp // tc
    per_b = lambda shape: pl.BlockSpec((1,) + shape, lambda b, c: (b,) + (0,) * len(shape))
    const = lambda shape: pl.BlockSpec(shape, lambda b, c: (0,) * len(shape))
    state_specs = (per_b((HEADS, M_DQK, M_DV)), per_b((HEADS, M_DQK)), per_b((HEADS, LANES)))
    return pl.pallas_call(
        functools.partial(_mlstm_chunk_kernel, seq_len=seq_len, tc=tc),
        out_shape=(jax.ShapeDtypeStruct((nb, lp, D_MODEL), bf16),
                   jax.ShapeDtypeStruct((nb, HEADS, M_DQK, M_DV), f32),
                   jax.ShapeDtypeStruct((nb, HEADS, M_DQK), f32),
                   jax.ShapeDtypeStruct((nb, HEADS, LANES), f32)),
        grid=(nb, nc),
        in_specs=[pl.BlockSpec((1, tc, 3 * D_MODEL), lambda b, c: (b, c, 0)),
                  pl.BlockSpec((1, tc, LANES), lambda b, c: (b, c, 0)),
                  pl.BlockSpec((1, 2 * HEADS, tc), lambda b, c: (b, 0, c)),
                  const((1, LANES)), const((2 * HEADS, 1)), const((1, D_MODEL))] + list(state_specs),
        out_specs=(pl.BlockSpec((1, tc, D_MODEL), lambda b, c: (b, c, 0)),) + state_specs,
        scratch_shapes=[pltpu.VMEM((HEADS, M_DQK, M_DV), f32), pltpu.VMEM((HEADS, M_DQK), f32),
                        pltpu.VMEM((HEADS, LANES), f32)],
        compiler_params=_params(("parallel", "arbitrary"), 32),
        name="mlstm_chunks",
    )(z, gcol, grow, bcol, brow, norm_g, c0t, n0, m0b)


ATT_TILE = 256
SAMPLE_TILE = 128
MLSTM_CHUNK = 128
PAGES_PER_STEP = 4
MOE_TILE = 1024
ROUTER_TILE = 512


def _attention_layer(xp, xp_b, xs, xs_b, cache_k, cache_v, page_table, w_in, lam_p, g, w_out, ln_g, ln_b, lam_init):
    nbd, n_new, _ = xs.shape
    seq_len = xp.shape[1]
    past = page_table.shape[1] * cache_k.shape[1]
    qb, kb, vb, kf, vf = _qkv_rope(xp_b, w_in, jnp.arange(seq_len, dtype=jnp.int32), ATT_TILE)
    o = _flash_attention(qb, kb, vb, lam_p, g, lam_init, ATT_TILE)
    xp1, xp1_b = _proj_ln(o, xp, w_out, ln_g, ln_b, ATT_TILE)
    pos_s = past + (jnp.arange(nbd * n_new, dtype=jnp.int32) % n_new)
    xs_rows = xs_b.reshape(1, nbd * n_new, D_MODEL)
    qs, _, _, ksf, vsf = _qkv_rope(xs_rows, w_in, pos_s, SAMPLE_TILE)
    qs = qs[:, :nbd * n_new].reshape(nbd, n_new, D_MODEL)
    ksf = ksf.reshape(nbd, n_new, D_MODEL)
    vsf = vsf.reshape(nbd, n_new, D_MODEL)
    ck = cache_k.reshape(cache_k.shape[0], cache_k.shape[1], D_MODEL)
    cv = cache_v.reshape(cache_v.shape[0], cache_v.shape[1], D_MODEL)
    os_ = _decode_attention(qs, ksf, vsf, ck, cv, page_table, lam_p, g, lam_init, PAGES_PER_STEP)
    os_b = os_.astype(bf16).reshape(1, nbd * n_new, D_MODEL)
    xs1, xs1_b = _proj_ln(os_b, xs.reshape(1, nbd * n_new, D_MODEL), w_out, ln_g, ln_b, SAMPLE_TILE)
    return (xp1, xp1_b, xs1.reshape(xs.shape), xs1_b.reshape(xs.shape),
            kf.reshape(kf.shape[:2] + (HEADS, HEAD_W)), vf.reshape(vf.shape[:2] + (HEADS, HEAD_W)),
            ksf.reshape(nbd, n_new, HEADS, HEAD_W), vsf.reshape(nbd, n_new, HEADS, HEAD_W))


def _mlstm_stream(x, x_b, w_main, wg_cols, wg_rows, bcol, brow, norm_g, c0t, n0, m0b, w_out, ln_g, ln_b):
    nb, seq_len, _ = x.shape
    short = seq_len < MLSTM_CHUNK
    if short:
        rows = nb * seq_len
        z, gcol, grow = _mlstm_proj(x_b.reshape(1, rows, D_MODEL), w_main, wg_cols, wg_rows, rows)
        pad = MLSTM_CHUNK - seq_len
        z = jnp.pad(z[0, :rows].reshape(nb, seq_len, 3 * D_MODEL), [(0, 0), (0, pad), (0, 0)])
        gcol = jnp.pad(gcol[0, :rows].reshape(nb, seq_len, LANES), [(0, 0), (0, pad), (0, 0)])
        grow = jnp.transpose(grow[0, :, :rows].reshape(2 * HEADS, nb, seq_len), (1, 0, 2))
        grow = jnp.pad(grow, [(0, 0), (0, 0), (0, pad)])
    else:
        z, gcol, grow = _mlstm_proj(x_b, w_main, wg_cols, wg_rows, MLSTM_CHUNK)
    hb, ct, n, mb = _mlstm_chunks(z, gcol, grow, bcol, brow, norm_g, c0t, n0, m0b, seq_len, MLSTM_CHUNK)
    if short:
        hb = hb[:, :seq_len].reshape(1, rows, D_MODEL)
        x1, x1_b = _proj_ln(hb, x.reshape(1, rows, D_MODEL), w_out, ln_g, ln_b, rows)
        x1, x1_b = x1.reshape(x.shape), x1_b.reshape(x.shape)
    else:
        x1, x1_b = _proj_ln(hb, x, w_out, ln_g, ln_b, ATT_TILE)
    return x1, x1_b, jnp.swapaxes(ct, -1, -2), n, mb[..., 0]


def kernel(x_prompt, x_sample, cache_k, cache_v, state_C, state_n, state_m, page_table, meta_tokens, w_attn_in, lambda_q1, lambda_k1, lambda_q2, lambda_k2, subln_g, w_attn_out, w_mlstm_in, b_mlstm_if, mlstm_norm_g, w_mlstm_out, w_router, b_router, w_exp_gate, w_exp_up, w_exp_down, ln_g, ln_b):
    nb = x_prompt.shape[0]
    nbd = x_sample.shape[0]
    meta = jnp.broadcast_to(meta_tokens.astype(x_prompt.dtype)[None], (nb, N_META, D_MODEL))
    xp = jnp.concatenate([meta, x_prompt], axis=1)
    xs = x_sample
    xp_b = xp.astype(bf16)
    xs_b = xs.astype(bf16)
    router_w = _split_hi_lo(w_router.T) + (b_router.astype(f32).reshape(N_EXPERTS, 1),)
    prompt_moe_tile = min(MOE_TILE, xp.shape[0] * xp.shape[1])
    sample_rows = nbd * xs.shape[1]

    def moe_pair(i, xp, xp_b, xs, xs_b):
        args = (router_w, w_exp_gate[i], w_exp_up[i], w_exp_down[i], ln_g[i, 1:2], ln_b[i, 1:2])
        xp, xp_b = _moe_block(xp, xp_b, *args, ROUTER_TILE, prompt_moe_tile)
        xs, xs_b = _moe_block(xs, xs_b, *args, sample_rows, sample_rows)
        return xp, xp_b, xs, xs_b

    lam_init = 0.8 - 0.6 * math.exp(-0.3 * 0)
    lam_p = jnp.stack([lambda_q1[0], lambda_k1[0], lambda_q2[0], lambda_k2[0]]).astype(f32)
    (xp, xp_b, xs, xs_b, k_p, v_p, k_s, v_s) = _attention_layer(
        xp, xp_b, xs, xs_b, cache_k[0], cache_v[0], page_table, w_attn_in[0].astype(bf16), lam_p,
        subln_g[0].reshape(1, HEAD_W), w_attn_out[0].astype(bf16), ln_g[0, 0:1], ln_b[0, 0:1], lam_init)
    xp, xp_b, xs, xs_b = moe_pair(0, xp, xp_b, xs, xs_b)

    w_in = w_mlstm_in[0]
    w_main = w_in[:, :3 * D_MODEL].astype(bf16)
    w_gate = w_in[:, 3 * D_MODEL:]
    wg_cols = _split_hi_lo(jnp.pad(w_gate, [(0, 0), (0, LANES - 2 * HEADS)]))
    wg_rows = _split_hi_lo(w_gate.T)
    b_if = b_mlstm_if[0].astype(f32).reshape(2 * HEADS)
    bcol = jnp.pad(b_if, (0, LANES - 2 * HEADS)).reshape(1, LANES)
    brow = b_if.reshape(2 * HEADS, 1)
    norm_g = mlstm_norm_g[0].reshape(1, D_MODEL)
    common = (w_main, wg_cols, wg_rows, bcol, brow, norm_g)
    tail = (w_mlstm_out[0].astype(bf16), ln_g[1, 0:1], ln_b[1, 0:1])
    zero_state = (jnp.zeros((nb, HEADS, M_DQK, M_DV), f32), jnp.zeros((nb, HEADS, M_DQK), f32),
                  jnp.zeros((nb, HEADS, LANES), f32))
    xp, xp_b, c_p, n_p, m_p = _mlstm_stream(xp, xp_b, *common, *zero_state, *tail)
    sample_state = (jnp.swapaxes(state_C[0].astype(f32), -1, -2), state_n[0].astype(f32),
                    jnp.broadcast_to(state_m[0].astype(f32)[..., None], (nbd, HEADS, LANES)))
    xs, xs_b, c_s, n_s, m_s = _mlstm_stream(xs, xs_b, *common, *sample_state, *tail)
    xp, xp_b, xs, xs_b = moe_pair(1, xp, xp_b, xs, xs_b)

    return (xp[:, N_META:], xs, k_p[None], v_p[None], k_s[None], v_s[None],
            c_p[None], n_p[None], m_p[None], c_s[None], n_s[None], m_s[None])
```

```python
import functools
import math

import jax
import jax.numpy as jnp
from jax import lax
from jax.experimental import pallas as pl
from jax.experimental.pallas import tpu as pltpu

f32 = jnp.float32
bf16 = jnp.bfloat16

D_MODEL = 1024
N_META = 16
HEADS = 8
HEAD_W = 128
SUB_W = 64
ATT_SCALE = SUB_W ** -0.5
Q_SCALE = ATT_SCALE * math.log2(math.e)
ROT_DIM = 16
ROPE_THETA = 500000.0
M_DQK = 64
M_DV = 128
GATE_SOFTCAP = 15.0
N_EXPERTS = 16
N_GROUPS = 4
EXPERTS_PER_GROUP = 4
D_EXPERT = 512
DEPTH = 2
DN_ALPHA = (2 * DEPTH) ** 0.25
LN_EPS = 1e-5
RMS_EPS = 1e-6
NEG_BIG = -1e30

LANES = 128
MIB = 1024 * 1024

NT_DIMS = (((1,), (1,)), ((), ()))
TN_DIMS = (((0,), (0,)), ((), ()))


def _params(semantics, vmem_mib):
    return pltpu.CompilerParams(dimension_semantics=semantics, vmem_limit_bytes=vmem_mib * MIB)


def _split_hi_lo(a):
    hi = a.astype(bf16)
    lo = (a - hi.astype(f32)).astype(bf16)
    return hi, lo


def _matmul(x, w):
    if x.dtype == bf16:
        return jnp.dot(x, w, preferred_element_type=f32)
    xh, xl = _split_hi_lo(x)
    wh, wl = _split_hi_lo(w)
    return (jnp.dot(xh, wh, preferred_element_type=f32)
            + (jnp.dot(xl, wh, preferred_element_type=f32) + jnp.dot(xh, wl, preferred_element_type=f32)))


def _layer_norm(r, g, b):
    mu = jnp.mean(r, axis=-1, keepdims=True)
    rc = r - mu
    var = jnp.mean(rc * rc, axis=-1, keepdims=True)
    return rc * lax.rsqrt(var + LN_EPS) * g + b


def _qkv_rope_kernel(x_ref, w_ref, c_ref, s1_ref, s2_ref,
                     qb_ref, kb_ref, vb_ref, kf_ref, vf_ref, *maybe_qf_ref, seq_len, tm):
    i = pl.program_id(1)
    rows = i * tm + lax.broadcasted_iota(jnp.int32, (tm, 1), 0)
    valid = rows < seq_len
    x = x_ref[0]
    c = c_ref[...]
    s1 = s1_ref[...]
    s2 = s2_ref[...]
    for part in range(3):
        y = _matmul(x, w_ref[:, part * D_MODEL:(part + 1) * D_MODEL])
        for h in range(HEADS):
            sl = slice(h * HEAD_W, (h + 1) * HEAD_W)
            yh = y[:, sl]
            if part < 2:
                yh = yh * c + pltpu.roll(yh, LANES - ROT_DIM // 2, 1) * s1 + pltpu.roll(yh, ROT_DIM // 2, 1) * s2
            if part == 0:
                qb_ref[0, :, sl] = jnp.where(valid, yh * Q_SCALE, 0.0).astype(bf16)
                for qf_ref in maybe_qf_ref:
                    qf_ref[0, :, sl] = yh * Q_SCALE
            elif part == 1:
                kf_ref[0, :, sl] = yh
                kb_ref[0, :, sl] = jnp.where(valid, yh, 0.0).astype(bf16)
            else:
                vf_ref[0, :, sl] = yh
                vb_ref[0, :, sl] = jnp.where(valid, yh, 0.0).astype(bf16)


def _rope_tables(pos):
    half = ROT_DIM // 2
    inv_freq = ROPE_THETA ** (-jnp.arange(half, dtype=f32) * (2.0 / ROT_DIM))
    ang = pos.astype(f32)[:, None] * inv_freq[None, :]
    cos = jnp.cos(ang)
    sin = jnp.sin(ang)
    t = pos.shape[0]
    ones = jnp.ones((t, SUB_W - ROT_DIM), f32)
    zeros8 = jnp.zeros((t, half), f32)
    zeros48 = jnp.zeros((t, SUB_W - ROT_DIM), f32)
    c = jnp.concatenate([cos, cos, ones], axis=1)
    s1 = jnp.concatenate([-sin, zeros8, zeros48], axis=1)
    s2 = jnp.concatenate([zeros8, sin, zeros48], axis=1)
    return tuple(jnp.concatenate([a, a], axis=1) for a in (c, s1, s2))


def _qkv_rope(x, w, pos, tm, q_f32=False):
    nb, seq_len, _ = x.shape
    nt = pl.cdiv(seq_len, tm)
    lp = nt * tm
    c, s1, s2 = _rope_tables(pos)
    row_spec = pl.BlockSpec((1, tm, D_MODEL), lambda b, i: (b, i, 0))
    tab_spec = pl.BlockSpec((tm, HEAD_W), lambda b, i: (i, 0))
    pad_shape = jax.ShapeDtypeStruct((nb, lp, D_MODEL), bf16)
    out_shape = jax.ShapeDtypeStruct((nb, seq_len, D_MODEL), f32)
    n_out = 6 if q_f32 else 5
    return pl.pallas_call(
        functools.partial(_qkv_rope_kernel, seq_len=seq_len, tm=tm),
        out_shape=(pad_shape, pad_shape, pad_shape) + (out_shape,) * (n_out - 3),
        grid=(nb, nt),
        in_specs=[row_spec, pl.BlockSpec((D_MODEL, 3 * D_MODEL), lambda b, i: (0, 0)), tab_spec, tab_spec, tab_spec],
        out_specs=(row_spec,) * n_out,
        compiler_params=_params(("parallel", "parallel"), 48),
        name="qkv_rope",
    )(x, w, c, s1, s2)


def _diff_lambda(lam_ref, lam_init):
    p = lam_ref[...]
    l1 = jnp.sum(p[0:1] * p[1:2], axis=-1, keepdims=True)
    l2 = jnp.sum(p[2:3] * p[3:4], axis=-1, keepdims=True)
    return jnp.exp(l1) - jnp.exp(l2) + lam_init


def _diff_finish(o1, o2, lam, g, lam_init):
    o = o1 - lam * o2
    o = o * lax.rsqrt(jnp.mean(o * o, axis=-1, keepdims=True) + RMS_EPS)
    return o * g * (1.0 - lam_init)


def _flash_kernel(lam_ref, g_ref, q_ref, k_ref, v_ref, o_ref, qq_sc, m_sc, l_sc, acc_sc, *, lam_init, tq):
    qi = pl.program_id(2)
    q = q_ref[0]
    lane = lax.broadcasted_iota(jnp.int32, (tq, HEAD_W), 1)
    zero = jnp.zeros_like(q)
    qq_sc[0:tq, :] = jnp.where(lane < SUB_W, q, zero)
    qq_sc[tq:2 * tq, :] = jnp.where(lane >= SUB_W, q, zero)
    m_sc[...] = jnp.full(m_sc.shape, NEG_BIG, f32)
    l_sc[...] = jnp.zeros(l_sc.shape, f32)
    acc_sc[...] = jnp.zeros(acc_sc.shape, f32)

    def step(kstart, masked):
        k = k_ref[0, pl.ds(kstart, tq), :]
        v = v_ref[0, pl.ds(kstart, tq), :]
        s = lax.dot_general(qq_sc[...], k, NT_DIMS, preferred_element_type=f32)
        if masked:
            r = lax.broadcasted_iota(jnp.int32, (2 * tq, tq), 0)
            r = jnp.where(r >= tq, r - tq, r)
            cidx = lax.broadcasted_iota(jnp.int32, (2 * tq, tq), 1)
            s = jnp.where(cidx <= r, s, NEG_BIG)
        m_prev = m_sc[...]
        m_new = jnp.maximum(m_prev, jnp.max(s, axis=-1, keepdims=True))
        alpha = jnp.exp2(m_prev - m_new)
        parts = [jnp.exp2(s[:, j * LANES:(j + 1) * LANES] - m_new) for j in range(tq // LANES)]
        psum = parts[0]
        for part in parts[1:]:
            psum = psum + part
        p = jnp.concatenate([part.astype(bf16) for part in parts], axis=1)
        l_sc[...] = alpha * l_sc[...] + psum
        acc_sc[...] = alpha * acc_sc[...] + jnp.dot(p, v, preferred_element_type=f32)
        m_sc[...] = m_new

    def body(ki, carry):
        step(pl.multiple_of(ki * tq, tq), False)
        return carry

    lax.fori_loop(0, qi, body, 0)
    step(pl.multiple_of(qi * tq, tq), True)

    lam = _diff_lambda(lam_ref, lam_init)
    l = jnp.sum(l_sc[...], axis=-1, keepdims=True)
    o1 = acc_sc[0:tq, :] / l[0:tq, :]
    o2 = acc_sc[tq:2 * tq, :] / l[tq:2 * tq, :]
    o_ref[0] = _diff_finish(o1, o2, lam, g_ref[...], lam_init).astype(bf16)


def _flash_attention(qb, kb, vb, lam_p, g, lam_init, tq):
    nb, lp, _ = qb.shape
    nq = lp // tq
    q_spec = pl.BlockSpec((1, tq, HEAD_W), lambda b, h, i: (b, i, h))
    kv_spec = pl.BlockSpec((1, lp, HEAD_W), lambda b, h, i: (b, 0, h))
    return pl.pallas_call(
        functools.partial(_flash_kernel, lam_init=lam_init, tq=tq),
        out_shape=jax.ShapeDtypeStruct((nb, lp, D_MODEL), bf16),
        grid=(nb, HEADS, nq),
        in_specs=[pl.BlockSpec((4, SUB_W), lambda b, h, i: (0, 0)),
                  pl.BlockSpec((1, HEAD_W), lambda b, h, i: (0, 0)),
                  q_spec, kv_spec, kv_spec],
        out_specs=q_spec,
        scratch_shapes=[pltpu.VMEM((2 * tq, HEAD_W), bf16), pltpu.VMEM((2 * tq, LANES), f32),
                        pltpu.VMEM((2 * tq, LANES), f32), pltpu.VMEM((2 * tq, HEAD_W), f32)],
        compiler_params=_params(("parallel", "parallel", "arbitrary"), 48),
        name="flash_diff_attention",
    )(lam_p, g, qb, kb, vb)


def _decode_kernel(pt_ref, lam_ref, g_ref, q_ref, kn_ref, vn_ref, *rest, lam_init, pages_per_step, n_new):
    k_refs = rest[:pages_per_step]
    v_refs = rest[pages_per_step:2 * pages_per_step]
    o_ref = rest[2 * pages_per_step]
    m_sc, l_sc, acc_sc = rest[2 * pages_per_step + 1:]
    del pt_ref
    p_idx = pl.program_id(1)
    n_rows = HEADS * 2 * n_new
    page = k_refs[0].shape[1]

    @pl.when(p_idx == 0)
    def _():
        m_sc[...] = jnp.full(m_sc.shape, NEG_BIG, f32)
        l_sc[...] = jnp.zeros(l_sc.shape, f32)
        acc_sc[...] = jnp.zeros(acc_sc.shape, f32)

    q2 = q_ref[0]
    q_hi = q2[0:n_rows]

    def step(kv_refs, masked):
        scores = []
        for k_ref, _ in kv_refs:
            k_hi, k_lo = _split_hi_lo(k_ref[0])
            a = lax.dot_general(q2, k_hi, NT_DIMS, preferred_element_type=f32)
            s = (a[0:n_rows] + a[n_rows:2 * n_rows]) + lax.dot_general(q_hi, k_lo, NT_DIMS,
                                                                       preferred_element_type=f32)
            if masked:
                r = lax.broadcasted_iota(jnp.int32, (n_rows, page), 0)
                tok = r % n_new
                cidx = lax.broadcasted_iota(jnp.int32, (n_rows, page), 1)
                s = jnp.where(cidx <= tok, s, NEG_BIG)
            scores.append(s)
        m_prev = m_sc[...]
        m_new = m_prev
        for s in scores:
            m_new = jnp.maximum(m_new, jnp.max(s, axis=-1, keepdims=True))
        alpha = jnp.exp2(m_prev - m_new)
        l_new = alpha * l_sc[...]
        pv = None
        for s, (_, v_ref) in zip(scores, kv_refs):
            p = jnp.exp2(s - m_new)
            l_new = l_new + jnp.sum(p, axis=-1, keepdims=True)
            p_hi, p_lo = _split_hi_lo(p)
            v_hi, v_lo = _split_hi_lo(v_ref[0])
            r = jnp.dot(jnp.concatenate([p_hi, p_lo], axis=0), v_hi, preferred_element_type=f32)
            part = (r[0:n_rows] + r[n_rows:2 * n_rows]) + jnp.dot(p_hi, v_lo, preferred_element_type=f32)
            pv = part if pv is None else pv + part
        l_sc[...] = l_new
        rph = 2 * n_new
        for h in range(HEADS):
            rs = slice(h * rph, (h + 1) * rph)
            acc_sc[rs, :] = alpha[rs, :] * acc_sc[rs, :] + pv[rs, h * HEAD_W:(h + 1) * HEAD_W]
        m_sc[...] = m_new

    step(list(zip(k_refs, v_refs)), False)

    @pl.when(p_idx == pl.num_programs(1) - 1)
    def _():
        step([(kn_ref, vn_ref)], True)
        lam = _diff_lambda(lam_ref, lam_init)
        o = acc_sc[...] / l_sc[...]
        rph = 2 * n_new
        for h in range(HEADS):
            o1 = o[h * rph:h * rph + n_new, :]
            o2 = o[h * rph + n_new:(h + 1) * rph, :]
            o_ref[0, :, h * HEAD_W:(h + 1) * HEAD_W] = _diff_finish(o1, o2, lam, g_ref[...], lam_init)


def _decode_attention(q, k_new, v_new, cache_k, cache_v, page_table, page_base, lam_p, g, lam_init,
                      pages_per_step):
    nbd, n_new, _ = q.shape
    page = cache_k.shape[1]
    n_pages = page_table.shape[1]
    n_rows = HEADS * 2 * n_new
    row_sub = jnp.arange(n_rows, dtype=jnp.int32) // n_new
    col_sub = jnp.arange(D_MODEL, dtype=jnp.int32) // SUB_W
    q_rep = jnp.tile(q[:, None], (1, 2 * HEADS, 1, 1)).reshape(nbd, n_rows, D_MODEL)
    q_bd = jnp.where(row_sub[None, :, None] == col_sub[None, None, :], q_rep, jnp.zeros_like(q_rep))
    q_bd = jnp.concatenate(_split_hi_lo(q_bd), axis=1)
    pad = [(0, 0), (0, page - n_new), (0, 0)]
    kn = jnp.pad(k_new, pad)
    vn = jnp.pad(v_new, pad)
    steps = n_pages // pages_per_step

    def page_spec(j):
        return pl.BlockSpec((1, page, D_MODEL),
                            lambda b, p, pt: (page_base + pt[b, p * pages_per_step + j], 0, 0))

    per_seq = lambda rows: pl.BlockSpec((1, rows, D_MODEL), lambda b, p, pt: (b, 0, 0))
    grid_spec = pltpu.PrefetchScalarGridSpec(
        num_scalar_prefetch=1,
        grid=(nbd, steps),
        in_specs=[pl.BlockSpec((4, SUB_W), lambda b, p, pt: (0, 0)),
                  pl.BlockSpec((1, HEAD_W), lambda b, p, pt: (0, 0)),
                  per_seq(2 * n_rows), per_seq(page), per_seq(page)]
                 + [page_spec(j) for j in range(pages_per_step)]
                 + [page_spec(j) for j in range(pages_per_step)],
        out_specs=per_seq(n_new),
        scratch_shapes=[pltpu.VMEM((n_rows, 1), f32), pltpu.VMEM((n_rows, 1), f32),
                        pltpu.VMEM((n_rows, HEAD_W), f32)],
    )
    return pl.pallas_call(
        functools.partial(_decode_kernel, lam_init=lam_init, pages_per_step=pages_per_step, n_new=n_new),
        out_shape=jax.ShapeDtypeStruct((nbd, n_new, D_MODEL), f32),
        grid_spec=grid_spec,
        compiler_params=_params(("parallel", "arbitrary"), 40),
        name="paged_decode_attention",
    )(page_table, lam_p, g, q_bd, kn, vn, *([cache_k] * pages_per_step), *([cache_v] * pages_per_step))


def _proj_ln_kernel(o_ref, x_ref, w_ref, g_ref, b_ref, xo_ref, xb_ref):
    y = _matmul(o_ref[0], w_ref[...])
    out = _layer_norm(DN_ALPHA * x_ref[0] + y, g_ref[...], b_ref[...])
    xo_ref[0] = out
    xb_ref[0] = out.astype(bf16)


def _proj_ln(o, x, w, ln_g, ln_b, tm):
    nb, seq_len, _ = x.shape
    row_spec = pl.BlockSpec((1, tm, D_MODEL), lambda b, i: (b, i, 0))
    vec_spec = pl.BlockSpec((1, D_MODEL), lambda b, i: (0, 0))
    return pl.pallas_call(
        _proj_ln_kernel,
        out_shape=(jax.ShapeDtypeStruct(x.shape, f32), jax.ShapeDtypeStruct(x.shape, bf16)),
        grid=(nb, pl.cdiv(seq_len, tm)),
        in_specs=[row_spec, row_spec, pl.BlockSpec((D_MODEL, D_MODEL), lambda b, i: (0, 0)), vec_spec, vec_spec],
        out_specs=(row_spec, row_spec),
        compiler_params=_params(("parallel", "parallel"), 32),
        name="proj_residual_layernorm",
    )(o, x, w, ln_g, ln_b)


def _router_kernel(x_ref, wh_ref, wl_ref, b_ref, gate_ref):
    xh, xl = _split_hi_lo(x_ref[...])
    wh = wh_ref[...]
    logits = (lax.dot_general(wh, xh, NT_DIMS, preferred_element_type=f32)
              + (lax.dot_general(wh, xl, NT_DIMS, preferred_element_type=f32)
                 + lax.dot_general(wl_ref[...], xh, NT_DIMS, preferred_element_type=f32)))
    s = jax.nn.sigmoid(logits)
    sel = s + b_ref[...]
    rows = [sel[e:e + 1, :] for e in range(N_EXPERTS)]
    scores = []
    for gidx in range(N_GROUPS):
        r = rows[gidx * EXPERTS_PER_GROUP:(gidx + 1) * EXPERTS_PER_GROUP]
        best = None
        for a in range(EXPERTS_PER_GROUP):
            for b in range(a + 1, EXPERTS_PER_GROUP):
                pair = r[a] + r[b]
                best = pair if best is None else jnp.maximum(best, pair)
        scores.append(best)
    picked = []
    for e in range(N_EXPERTS):
        gidx, loc = divmod(e, EXPERTS_PER_GROUP)
        win = None
        for j in range(N_GROUPS):
            if j == gidx:
                continue
            c = scores[gidx] > scores[j] if j < gidx else scores[gidx] >= scores[j]
            win = c if win is None else jnp.logical_and(win, c)
        rank = jnp.zeros_like(rows[e])
        for j in range(EXPERTS_PER_GROUP):
            if j == loc:
                continue
            o = rows[gidx * EXPERTS_PER_GROUP + j]
            ahead = o >= rows[e] if j < loc else o > rows[e]
            rank = rank + ahead.astype(f32)
        chosen = jnp.logical_and(win, rank < 1.5)
        picked.append(jnp.where(chosen, s[e:e + 1, :], 0.0))
    total = picked[0]
    for e in range(1, N_EXPERTS):
        total = total + picked[e]
    for e in range(N_EXPERTS):
        gate_ref[e:e + 1, :] = picked[e] / total


def _router(x, w_hi, w_lo, b_col, tm):
    n_rows = x.shape[0]
    return pl.pallas_call(
        _router_kernel,
        out_shape=jax.ShapeDtypeStruct((N_EXPERTS, n_rows), f32),
        grid=(pl.cdiv(n_rows, tm),),
        in_specs=[pl.BlockSpec((tm, D_MODEL), lambda i: (i, 0)),
                  pl.BlockSpec((N_EXPERTS, D_MODEL), lambda i: (0, 0)),
                  pl.BlockSpec((N_EXPERTS, D_MODEL), lambda i: (0, 0)),
                  pl.BlockSpec((N_EXPERTS, 1), lambda i: (0, 0))],
        out_specs=pl.BlockSpec((N_EXPERTS, tm), lambda i: (0, i)),
        compiler_params=_params(("parallel",), 32),
        name="moe_router",
    )(x, w_hi, w_lo, b_col)


def _moe_kernel(xb_ref, x_ref, gate_ref, wg_ref, wu_ref, wd_ref, g_ref, b_ref, xo_ref, xbo_ref, acc_sc):
    e = pl.program_id(1)

    @pl.when(e == 0)
    def _():
        acc_sc[...] = jnp.zeros(acc_sc.shape, f32)

    x = xb_ref[...]
    a = jnp.dot(x, wg_ref[0].astype(bf16), preferred_element_type=f32)
    u = jnp.dot(x, wu_ref[0].astype(bf16), preferred_element_type=f32)
    hmid = (a * jax.nn.sigmoid(a) * u).astype(bf16)
    y = jnp.dot(hmid, wd_ref[0].astype(bf16), preferred_element_type=f32)
    gate = gate_ref[...]
    lane = lax.broadcasted_iota(jnp.int32, gate.shape, 1)
    gcol = jnp.sum(jnp.where(lane == e, gate, 0.0), axis=-1, keepdims=True)
    acc_sc[...] += gcol * y

    @pl.when(e == N_EXPERTS - 1)
    def _():
        out = _layer_norm(DN_ALPHA * x_ref[...] + acc_sc[...], g_ref[...], b_ref[...])
        xo_ref[...] = out
        xbo_ref[...] = out.astype(bf16)


def _moe_ln(x, x_b16, gate, wg, wu, wd, ln_g, ln_b, tm):
    n_rows = x.shape[0]
    row_spec = pl.BlockSpec((tm, D_MODEL), lambda i, e: (i, 0))
    vec_spec = pl.BlockSpec((1, D_MODEL), lambda i, e: (0, 0))
    return pl.pallas_call(
        _moe_kernel,
        out_shape=(jax.ShapeDtypeStruct(x.shape, f32), jax.ShapeDtypeStruct(x.shape, bf16)),
        grid=(pl.cdiv(n_rows, tm), N_EXPERTS),
        in_specs=[row_spec, row_spec, pl.BlockSpec((tm, N_EXPERTS), lambda i, e: (i, 0)),
                  pl.BlockSpec((1, D_MODEL, D_EXPERT), lambda i, e: (e, 0, 0)),
                  pl.BlockSpec((1, D_MODEL, D_EXPERT), lambda i, e: (e, 0, 0)),
                  pl.BlockSpec((1, D_EXPERT, D_MODEL), lambda i, e: (e, 0, 0)),
                  vec_spec, vec_spec],
        out_specs=(row_spec, row_spec),
        scratch_shapes=[pltpu.VMEM((tm, D_MODEL), f32)],
        compiler_params=_params(("parallel", "arbitrary"), 56),
        name="moe_experts_layernorm",
    )(x_b16, x, gate, wg, wu, wd, ln_g, ln_b)


def _moe_block(x, x_b16, router_w, wg, wu, wd, ln_g, ln_b, tm_router, tm_moe):
    shp = x.shape
    xf = x.reshape(-1, D_MODEL)
    xbf = x_b16.reshape(-1, D_MODEL)
    gate_t = _router(xf, *router_w, tm_router)
    xo, xbo = _moe_ln(xf, xbf, gate_t.T, wg, wu, wd, ln_g, ln_b, tm_moe)
    return xo.reshape(shp), xbo.reshape(shp)


def _mlstm_proj_kernel(x_ref, w_ref, wgh_ref, wgl_ref, wgth_ref, wgtl_ref, z_ref, gcol_ref, grow_ref, *, seq_len, tm):
    i = pl.program_id(1)
    rows = i * tm + lax.broadcasted_iota(jnp.int32, (tm, 1), 0)
    x = x_ref[0]
    x = jnp.where(rows < seq_len, x, jnp.zeros_like(x))
    for part in range(3):
        sl = slice(part * D_MODEL, (part + 1) * D_MODEL)
        z_ref[0, :, sl] = jnp.dot(x, w_ref[:, sl], preferred_element_type=f32).astype(bf16)
    gcol_ref[0] = (jnp.dot(x, wgh_ref[...], preferred_element_type=f32)
                   + jnp.dot(x, wgl_ref[...], preferred_element_type=f32))
    grow_ref[0] = (lax.dot_general(wgth_ref[...], x, NT_DIMS, preferred_element_type=f32)
                   + lax.dot_general(wgtl_ref[...], x, NT_DIMS, preferred_element_type=f32))


def _mlstm_proj(x_b16, w_main, wg_cols, wg_rows, tm):
    nb, seq_len, _ = x_b16.shape
    nt = pl.cdiv(seq_len, tm)
    lp = nt * tm
    const = lambda shape: pl.BlockSpec(shape, lambda b, i: (0,) * len(shape))
    return pl.pallas_call(
        functools.partial(_mlstm_proj_kernel, seq_len=seq_len, tm=tm),
        out_shape=(jax.ShapeDtypeStruct((nb, lp, 3 * D_MODEL), bf16),
                   jax.ShapeDtypeStruct((nb, lp, LANES), f32),
                   jax.ShapeDtypeStruct((nb, 2 * HEADS, lp), f32)),
        grid=(nb, nt),
        in_specs=[pl.BlockSpec((1, tm, D_MODEL), lambda b, i: (b, i, 0)),
                  const((D_MODEL, 3 * D_MODEL)), const((D_MODEL, LANES)), const((D_MODEL, LANES)),
                  const((2 * HEADS, D_MODEL)), const((2 * HEADS, D_MODEL))],
        out_specs=(pl.BlockSpec((1, tm, 3 * D_MODEL), lambda b, i: (b, i, 0)),
                   pl.BlockSpec((1, tm, LANES), lambda b, i: (b, i, 0)),
                   pl.BlockSpec((1, 2 * HEADS, tm), lambda b, i: (b, 0, i))),
        compiler_params=_params(("parallel", "parallel"), 48),
        name="mlstm_in_proj",
    )(x_b16, w_main, *wg_cols, *wg_rows)


def _softcap(z):
    return GATE_SOFTCAP * jnp.tanh(z / GATE_SOFTCAP)


def _log_sigmoid(z):
    return jnp.minimum(z, 0.0) - jnp.log(1.0 + jnp.exp(-jnp.abs(z)))


def _mlstm_chunk_kernel(z_ref, gcol_ref, grow_ref, bcol_ref, brow_ref, g_ref, c0_ref, n0_ref, m0_ref,
                        h_ref, c_ref, n_ref, m_ref, c_sc, n_sc, m_sc, *, seq_len, tc):
    ci = pl.program_id(1)

    @pl.when(ci == 0)
    def _():
        c_sc[...] = c0_ref[0]
        n_sc[...] = n0_ref[0]
        m_sc[...] = m0_ref[0]

    t_col = ci * tc + lax.broadcasted_iota(jnp.int32, (tc, LANES), 0)
    lane = lax.broadcasted_iota(jnp.int32, (tc, LANES), 1)
    pre = _softcap(gcol_ref[0] + bcol_ref[...])
    gates_c = jnp.where(lane < HEADS, pre, _log_sigmoid(pre))
    neutral_c = jnp.where(lane < HEADS, NEG_BIG, 0.0)
    gates_c = jnp.where(t_col < seq_len, gates_c, neutral_c)
    t_row = ci * tc + lax.broadcasted_iota(jnp.int32, (2 * HEADS, tc), 1)
    sub = lax.broadcasted_iota(jnp.int32, (2 * HEADS, tc), 0)
    pre_r = _softcap(grow_ref[0] + brow_ref[...])
    gates_r = jnp.where(sub < HEADS, pre_r, _log_sigmoid(pre_r))
    neutral_r = jnp.where(sub < HEADS, NEG_BIG, 0.0)
    gates_r = jnp.where(t_row < seq_len, gates_r, neutral_r)

    ri = lax.broadcasted_iota(jnp.int32, (tc, tc), 0)
    cj = lax.broadcasted_iota(jnp.int32, (tc, tc), 1)
    causal = cj <= ri
    tril = jnp.where(causal, 1.0, 0.0).astype(bf16)
    triu = jnp.where(ri <= cj, 1.0, 0.0).astype(bf16)
    hi, lo = _split_hi_lo(gates_c)
    cum_c = jnp.dot(tril, hi, preferred_element_type=f32) + jnp.dot(tril, lo, preferred_element_type=f32)
    hi, lo = _split_hi_lo(gates_r)
    cum_r = jnp.dot(hi, triu, preferred_element_type=f32) + jnp.dot(lo, triu, preferred_element_type=f32)

    k_scale = M_DQK ** -0.5
    for h in range(HEADS):
        q = z_ref[0, :, h * M_DQK:(h + 1) * M_DQK]
        k = z_ref[0, :, HEADS * M_DQK + h * M_DQK:HEADS * M_DQK + (h + 1) * M_DQK]
        v = z_ref[0, :, D_MODEL + h * M_DV:D_MODEL + (h + 1) * M_DV]
        og = z_ref[0, :, 2 * D_MODEL + h * M_DV:2 * D_MODEL + (h + 1) * M_DV]
        b_col = cum_c[:, HEADS + h:HEADS + h + 1]
        a_col = gates_c[:, h:h + 1] - b_col
        a_row = gates_r[h:h + 1, :] - cum_r[HEADS + h:HEADS + h + 1, :]
        m0 = m_sc[h:h + 1, 0:1]
        c0 = c_sc[h]
        n0 = n_sc[h:h + 1, :]

        amat = jnp.where(causal, a_row, NEG_BIG)
        u = jnp.maximum(m0, jnp.max(amat, axis=-1, keepdims=True))
        dmat = jnp.exp(amat - u)
        s = lax.dot_general(q, k, NT_DIMS, preferred_element_type=f32) * k_scale
        w = s * dmat
        g_in = jnp.exp(m0 - u)
        num = (jnp.dot(w.astype(bf16), v, preferred_element_type=f32)
               + g_in * jnp.dot(q, c0.astype(bf16), preferred_element_type=f32))
        qn = jnp.sum(q.astype(f32) * n0, axis=-1, keepdims=True)
        den = jnp.sum(w, axis=-1, keepdims=True) + g_in * qn
        m_tok = b_col + u
        hval = num / jnp.maximum(jnp.abs(den), jnp.exp(-m_tok))
        hval = hval * lax.rsqrt(jnp.mean(hval * hval, axis=-1, keepdims=True) + RMS_EPS)
        hval = hval * g_ref[:, h * M_DV:(h + 1) * M_DV] * jax.nn.sigmoid(og.astype(f32))
        h_ref[0, :, h * M_DV:(h + 1) * M_DV] = hval.astype(bf16)

        u_last = u[tc - 1:tc, :]
        g0 = jnp.exp(m0 - u_last)
        decay = jnp.exp(a_col - u_last)
        kd = k.astype(f32) * (decay * k_scale)
        c_sc[h] = g0 * c0 + lax.dot_general(kd.astype(bf16), v, TN_DIMS, preferred_element_type=f32)
        n_sc[h:h + 1, :] = g0 * n0 + jnp.sum(kd, axis=0, keepdims=True)
        m_sc[h:h + 1, :] = jnp.broadcast_to(b_col[tc - 1:tc, :] + u_last, (1, LANES))

    @pl.when(ci == pl.num_programs(1) - 1)
    def _():
        c_ref[0] = c_sc[...]
        n_ref[0] = n_sc[...]
        m_ref[0] = m_sc[...]


def _mlstm_chunks(z, gcol, grow, bcol, brow, norm_g, c0t, n0, m0b, seq_len, tc):
    nb, lp, _ = z.shape
    nc = lp // tc
    per_b = lambda shape: pl.BlockSpec((1,) + shape, lambda b, c: (b,) + (0,) * len(shape))
    const = lambda shape: pl.BlockSpec(shape, lambda b, c: (0,) * len(shape))
    state_specs = (per_b((HEADS, M_DQK, M_DV)), per_b((HEADS, M_DQK)), per_b((HEADS, LANES)))
    return pl.pallas_call(
        functools.partial(_mlstm_chunk_kernel, seq_len=seq_len, tc=tc),
        out_shape=(jax.ShapeDtypeStruct((nb, lp, D_MODEL), bf16),
                   jax.ShapeDtypeStruct((nb, HEADS, M_DQK, M_DV), f32),
                   jax.ShapeDtypeStruct((nb, HEADS, M_DQK), f32),
                   jax.ShapeDtypeStruct((nb, HEADS, LANES), f32)),
        grid=(nb, nc),
        in_specs=[pl.BlockSpec((1, tc, 3 * D_MODEL), lambda b, c: (b, c, 0)),
                  pl.BlockSpec((1, tc, LANES), lambda b, c: (b, c, 0)),
                  pl.BlockSpec((1, 2 * HEADS, tc), lambda b, c: (b, 0, c)),
                  const((1, LANES)), const((2 * HEADS, 1)), const((1, D_MODEL))] + list(state_specs),
        out_specs=(pl.BlockSpec((1, tc, D_MODEL), lambda b, c: (b, c, 0)),) + state_specs,
        scratch_shapes=[pltpu.VMEM((HEADS, M_DQK, M_DV), f32), pltpu.VMEM((HEADS, M_DQK), f32),
                        pltpu.VMEM((HEADS, LANES), f32)],
        compiler_params=_params(("parallel", "arbitrary"), 32),
        name="mlstm_chunks",
    )(z, gcol, grow, bcol, brow, norm_g, c0t, n0, m0b)


ATT_TILE = 512
SAMPLE_TILE = 128
MLSTM_CHUNK = 128
PAGES_PER_STEP = 4
MOE_TILE = 1024
ROUTER_TILE = 512


def _attention_layer(xp, xp_b, xs, cache_k, cache_v, layer, page_table, w_in, lam_p, g, w_out, ln_g, ln_b, lam_init):
    nbd, n_new, _ = xs.shape
    seq_len = xp.shape[1]
    past = page_table.shape[1] * cache_k.shape[2]
    qb, kb, vb, kf, vf = _qkv_rope(xp_b, w_in.astype(bf16), jnp.arange(seq_len, dtype=jnp.int32), ATT_TILE)
    o = _flash_attention(qb, kb, vb, lam_p, g, lam_init, ATT_TILE)
    xp1, xp1_b = _proj_ln(o, xp, w_out.astype(bf16), ln_g, ln_b, ATT_TILE)
    pos_s = past + (jnp.arange(nbd * n_new, dtype=jnp.int32) % n_new)
    xs_rows = xs.reshape(1, nbd * n_new, D_MODEL)
    _, _, _, ksf, vsf, qs = _qkv_rope(xs_rows, w_in, pos_s, SAMPLE_TILE, q_f32=True)
    qs = qs.reshape(nbd, n_new, D_MODEL)
    ksf = ksf.reshape(nbd, n_new, D_MODEL)
    vsf = vsf.reshape(nbd, n_new, D_MODEL)
    n_pool, page = cache_k.shape[1], cache_k.shape[2]
    ck = cache_k.reshape(-1, page, D_MODEL)
    cv = cache_v.reshape(-1, page, D_MODEL)
    os_ = _decode_attention(qs, ksf, vsf, ck, cv, page_table, layer * n_pool, lam_p, g, lam_init, PAGES_PER_STEP)
    os_rows = os_.reshape(1, nbd * n_new, D_MODEL)
    xs1, xs1_b = _proj_ln(os_rows, xs_rows, w_out, ln_g, ln_b, SAMPLE_TILE)
    return (xp1, xp1_b, xs1.reshape(xs.shape), xs1_b.reshape(xs.shape),
            kf.reshape(kf.shape[:2] + (HEADS, HEAD_W)), vf.reshape(vf.shape[:2] + (HEADS, HEAD_W)),
            ksf.reshape(nbd, n_new, HEADS, HEAD_W), vsf.reshape(nbd, n_new, HEADS, HEAD_W))


def _mlstm_stream(x, x_b, w_main, wg_cols, wg_rows, bcol, brow, norm_g, c0t, n0, m0b, w_out, ln_g, ln_b):
    nb, seq_len, _ = x.shape
    short = seq_len < MLSTM_CHUNK
    if short:
        rows = nb * seq_len
        z, gcol, grow = _mlstm_proj(x_b.reshape(1, rows, D_MODEL), w_main, wg_cols, wg_rows, rows)
        pad = MLSTM_CHUNK - seq_len
        z = jnp.pad(z[0, :rows].reshape(nb, seq_len, 3 * D_MODEL), [(0, 0), (0, pad), (0, 0)])
        gcol = jnp.pad(gcol[0, :rows].reshape(nb, seq_len, LANES), [(0, 0), (0, pad), (0, 0)])
        grow = jnp.transpose(grow[0, :, :rows].reshape(2 * HEADS, nb, seq_len), (1, 0, 2))
        grow = jnp.pad(grow, [(0, 0), (0, 0), (0, pad)])
    else:
        z, gcol, grow = _mlstm_proj(x_b, w_main, wg_cols, wg_rows, MLSTM_CHUNK)
    hb, ct, n, mb = _mlstm_chunks(z, gcol, grow, bcol, brow, norm_g, c0t, n0, m0b, seq_len, MLSTM_CHUNK)
    if short:
        hb = hb[:, :seq_len].reshape(1, rows, D_MODEL)
        x1, x1_b = _proj_ln(hb, x.reshape(1, rows, D_MODEL), w_out, ln_g, ln_b, rows)
        x1, x1_b = x1.reshape(x.shape), x1_b.reshape(x.shape)
    else:
        x1, x1_b = _proj_ln(hb, x, w_out, ln_g, ln_b, ATT_TILE)
    return x1, x1_b, jnp.swapaxes(ct, -1, -2), n, mb[..., 0]


def kernel(x_prompt, x_sample, cache_k, cache_v, state_C, state_n, state_m, page_table, meta_tokens, w_attn_in, lambda_q1, lambda_k1, lambda_q2, lambda_k2, subln_g, w_attn_out, w_mlstm_in, b_mlstm_if, mlstm_norm_g, w_mlstm_out, w_router, b_router, w_exp_gate, w_exp_up, w_exp_down, ln_g, ln_b):
    nb = x_prompt.shape[0]
    nbd = x_sample.shape[0]
    meta = jnp.broadcast_to(meta_tokens.astype(x_prompt.dtype)[None], (nb, N_META, D_MODEL))
    xp = jnp.concatenate([meta, x_prompt], axis=1)
    xs = x_sample
    xp_b = xp.astype(bf16)
    router_w = _split_hi_lo(w_router.T) + (b_router.astype(f32).reshape(N_EXPERTS, 1),)
    prompt_moe_tile = min(MOE_TILE, xp.shape[0] * xp.shape[1])
    sample_rows = nbd * xs.shape[1]

    def moe_pair(i, xp, xp_b, xs, xs_b):
        args = (router_w, w_exp_gate[i], w_exp_up[i], w_exp_down[i], ln_g[i, 1:2], ln_b[i, 1:2])
        xp, xp_b = _moe_block(xp, xp_b, *args, ROUTER_TILE, prompt_moe_tile)
        xs, xs_b = _moe_block(xs, xs_b, *args, sample_rows, sample_rows)
        return xp, xp_b, xs, xs_b

    lam_init = 0.8 - 0.6 * math.exp(-0.3 * 0)
    lam_p = jnp.stack([lambda_q1[0], lambda_k1[0], lambda_q2[0], lambda_k2[0]]).astype(f32)
    (xp, xp_b, xs, xs_b, k_p, v_p, k_s, v_s) = _attention_layer(
        xp, xp_b, xs, cache_k, cache_v, 0, page_table, w_attn_in[0], lam_p,
        subln_g[0].reshape(1, HEAD_W), w_attn_out[0], ln_g[0, 0:1], ln_b[0, 0:1], lam_init)
    xp, xp_b, xs, xs_b = moe_pair(0, xp, xp_b, xs, xs_b)

    w_in = w_mlstm_in[0]
    w_main = w_in[:, :3 * D_MODEL].astype(bf16)
    w_gate = w_in[:, 3 * D_MODEL:]
    wg_cols = _split_hi_lo(jnp.pad(w_gate, [(0, 0), (0, LANES - 2 * HEADS)]))
    wg_rows = _split_hi_lo(w_gate.T)
    b_if = b_mlstm_if[0].astype(f32).reshape(2 * HEADS)
    bcol = jnp.pad(b_if, (0, LANES - 2 * HEADS)).reshape(1, LANES)
    brow = b_if.reshape(2 * HEADS, 1)
    norm_g = mlstm_norm_g[0].reshape(1, D_MODEL)
    common = (w_main, wg_cols, wg_rows, bcol, brow, norm_g)
    tail = (w_mlstm_out[0].astype(bf16), ln_g[1, 0:1], ln_b[1, 0:1])
    zero_state = (jnp.zeros((nb, HEADS, M_DQK, M_DV), f32), jnp.zeros((nb, HEADS, M_DQK), f32),
                  jnp.zeros((nb, HEADS, LANES), f32))
    xp, xp_b, c_p, n_p, m_p = _mlstm_stream(xp, xp_b, *common, *zero_state, *tail)
    sample_state = (jnp.swapaxes(state_C[0].astype(f32), -1, -2), state_n[0].astype(f32),
                    jnp.broadcast_to(state_m[0].astype(f32)[..., None], (nbd, HEADS, LANES)))
    xs, xs_b, c_s, n_s, m_s = _mlstm_stream(xs, xs_b, *common, *sample_state, *tail)
    xp, xp_b, xs, xs_b = moe_pair(1, xp, xp_b, xs, xs_b)

    return (xp[:, N_META:], xs, k_p[None], v_p[None], k_s[None], v_s[None],
            c_p[None], n_p[None], m_p[None], c_s[None], n_s[None], m_s[None])
```

```python
import functools
import math

import jax
import jax.numpy as jnp
from jax import lax
from jax.experimental import pallas as pl
from jax.experimental.pallas import tpu as pltpu

f32 = jnp.float32
bf16 = jnp.bfloat16

D_MODEL = 1024
N_META = 16
HEADS = 8
HEAD_W = 128
SUB_W = 64
ATT_SCALE = SUB_W ** -0.5
Q_SCALE = ATT_SCALE * math.log2(math.e)
ROT_DIM = 16
ROPE_THETA = 500000.0
M_DQK = 64
M_DV = 128
GATE_SOFTCAP = 15.0
N_EXPERTS = 16
N_GROUPS = 4
EXPERTS_PER_GROUP = 4
D_EXPERT = 512
DEPTH = 2
DN_ALPHA = (2 * DEPTH) ** 0.25
LN_EPS = 1e-5
RMS_EPS = 1e-6
NEG_BIG = -1e30

LANES = 128
MIB = 1024 * 1024

NT_DIMS = (((1,), (1,)), ((), ()))
TN_DIMS = (((0,), (0,)), ((), ()))


def _params(semantics, vmem_mib):
    return pltpu.CompilerParams(dimension_semantics=semantics, vmem_limit_bytes=vmem_mib * MIB)


def _split_hi_lo(a):
    hi = a.astype(bf16)
    lo = (a - hi.astype(f32)).astype(bf16)
    return hi, lo


def _matmul(x, w):
    if x.dtype == bf16:
        return jnp.dot(x, w, preferred_element_type=f32)
    xh, xl = _split_hi_lo(x)
    wh, wl = _split_hi_lo(w)
    return (jnp.dot(xh, wh, preferred_element_type=f32)
            + (jnp.dot(xl, wh, preferred_element_type=f32) + jnp.dot(xh, wl, preferred_element_type=f32)))


def _layer_norm(r, g, b):
    mu = jnp.mean(r, axis=-1, keepdims=True)
    rc = r - mu
    var = jnp.mean(rc * rc, axis=-1, keepdims=True)
    return rc * lax.rsqrt(var + LN_EPS) * g + b


def _qkv_rope_kernel(x_ref, w_ref, c_ref, s1_ref, s2_ref,
                     qb_ref, kb_ref, vb_ref, kf_ref, vf_ref, *maybe_qf_ref, seq_len, tm):
    i = pl.program_id(1)
    rows = i * tm + lax.broadcasted_iota(jnp.int32, (tm, 1), 0)
    valid = rows < seq_len
    x = x_ref[0]
    c = c_ref[...]
    s1 = s1_ref[...]
    s2 = s2_ref[...]
    for part in range(3):
        y = _matmul(x, w_ref[:, part * D_MODEL:(part + 1) * D_MODEL])
        for h in range(HEADS):
            sl = slice(h * HEAD_W, (h + 1) * HEAD_W)
            yh = y[:, sl]
            if part < 2:
                yh = yh * c + pltpu.roll(yh, LANES - ROT_DIM // 2, 1) * s1 + pltpu.roll(yh, ROT_DIM // 2, 1) * s2
            if part == 0:
                qb_ref[0, :, sl] = jnp.where(valid, yh * Q_SCALE, 0.0).astype(bf16)
                for qf_ref in maybe_qf_ref:
                    qf_ref[0, :, sl] = yh * Q_SCALE
            elif part == 1:
                kf_ref[0, :, sl] = yh
                kb_ref[0, :, sl] = jnp.where(valid, yh, 0.0).astype(bf16)
            else:
                vf_ref[0, :, sl] = yh
                vb_ref[0, :, sl] = jnp.where(valid, yh, 0.0).astype(bf16)


def _rope_tables(pos):
    half = ROT_DIM // 2
    inv_freq = ROPE_THETA ** (-jnp.arange(half, dtype=f32) * (2.0 / ROT_DIM))
    ang = pos.astype(f32)[:, None] * inv_freq[None, :]
    cos = jnp.cos(ang)
    sin = jnp.sin(ang)
    t = pos.shape[0]
    ones = jnp.ones((t, SUB_W - ROT_DIM), f32)
    zeros8 = jnp.zeros((t, half), f32)
    zeros48 = jnp.zeros((t, SUB_W - ROT_DIM), f32)
    c = jnp.concatenate([cos, cos, ones], axis=1)
    s1 = jnp.concatenate([-sin, zeros8, zeros48], axis=1)
    s2 = jnp.concatenate([zeros8, sin, zeros48], axis=1)
    return tuple(jnp.concatenate([a, a], axis=1) for a in (c, s1, s2))


def _qkv_rope(x, w, pos, tm, q_f32=False):
    nb, seq_len, _ = x.shape
    nt = pl.cdiv(seq_len, tm)
    lp = nt * tm
    c, s1, s2 = _rope_tables(pos)
    row_spec = pl.BlockSpec((1, tm, D_MODEL), lambda b, i: (b, i, 0))
    tab_spec = pl.BlockSpec((tm, HEAD_W), lambda b, i: (i, 0))
    pad_shape = jax.ShapeDtypeStruct((nb, lp, D_MODEL), bf16)
    out_shape = jax.ShapeDtypeStruct((nb, seq_len, D_MODEL), f32)
    n_out = 6 if q_f32 else 5
    return pl.pallas_call(
        functools.partial(_qkv_rope_kernel, seq_len=seq_len, tm=tm),
        out_shape=(pad_shape, pad_shape, pad_shape) + (out_shape,) * (n_out - 3),
        grid=(nb, nt),
        in_specs=[row_spec, pl.BlockSpec((D_MODEL, 3 * D_MODEL), lambda b, i: (0, 0)), tab_spec, tab_spec, tab_spec],
        out_specs=(row_spec,) * n_out,
        compiler_params=_params(("parallel", "parallel"), 48),
        name="qkv_rope",
    )(x, w, c, s1, s2)


def _diff_lambda(lam_ref, lam_init):
    p = lam_ref[...]
    l1 = jnp.sum(p[0:1] * p[1:2], axis=-1, keepdims=True)
    l2 = jnp.sum(p[2:3] * p[3:4], axis=-1, keepdims=True)
    return jnp.exp(l1) - jnp.exp(l2) + lam_init


def _diff_finish(o1, o2, lam, g, lam_init):
    o = o1 - lam * o2
    o = o * lax.rsqrt(jnp.mean(o * o, axis=-1, keepdims=True) + RMS_EPS)
    return o * g * (1.0 - lam_init)


def _flash_kernel(lam_ref, g_ref, q_ref, k_ref, v_ref, o_ref, qq_sc, m_sc, l_sc, acc_sc, *, lam_init, tq):
    qi = pl.program_id(2)
    q = q_ref[0]
    lane = lax.broadcasted_iota(jnp.int32, (tq, HEAD_W), 1)
    zero = jnp.zeros_like(q)
    qq_sc[0:tq, :] = jnp.where(lane < SUB_W, q, zero)
    qq_sc[tq:2 * tq, :] = jnp.where(lane >= SUB_W, q, zero)
    m_sc[...] = jnp.full(m_sc.shape, NEG_BIG, f32)
    l_sc[...] = jnp.zeros(l_sc.shape, f32)
    acc_sc[...] = jnp.zeros(acc_sc.shape, f32)

    def step(kstart, masked):
        k = k_ref[0, pl.ds(kstart, tq), :]
        v = v_ref[0, pl.ds(kstart, tq), :]
        s = lax.dot_general(qq_sc[...], k, NT_DIMS, preferred_element_type=f32)
        if masked:
            r = lax.broadcasted_iota(jnp.int32, (2 * tq, tq), 0)
            r = jnp.where(r >= tq, r - tq, r)
            cidx = lax.broadcasted_iota(jnp.int32, (2 * tq, tq), 1)
            s = jnp.where(cidx <= r, s, NEG_BIG)
        m_prev = m_sc[...]
        m_new = jnp.maximum(m_prev, jnp.max(s, axis=-1, keepdims=True))
        alpha = jnp.exp2(m_prev - m_new)
        parts = [jnp.exp2(s[:, j * LANES:(j + 1) * LANES] - m_new) for j in range(tq // LANES)]
        psum = parts[0]
        for part in parts[1:]:
            psum = psum + part
        p = jnp.concatenate([part.astype(bf16) for part in parts], axis=1)
        l_sc[...] = alpha * l_sc[...] + psum
        acc_sc[...] = alpha * acc_sc[...] + jnp.dot(p, v, preferred_element_type=f32)
        m_sc[...] = m_new

    def body(ki, carry):
        step(pl.multiple_of(ki * tq, tq), False)
        return carry

    lax.fori_loop(0, qi, body, 0)
    step(pl.multiple_of(qi * tq, tq), True)

    lam = _diff_lambda(lam_ref, lam_init)
    l = jnp.sum(l_sc[...], axis=-1, keepdims=True)
    o1 = acc_sc[0:tq, :] / l[0:tq, :]
    o2 = acc_sc[tq:2 * tq, :] / l[tq:2 * tq, :]
    o_ref[0] = _diff_finish(o1, o2, lam, g_ref[...], lam_init).astype(bf16)


def _flash_attention(qb, kb, vb, lam_p, g, lam_init, tq):
    nb, lp, _ = qb.shape
    nq = lp // tq
    q_spec = pl.BlockSpec((1, tq, HEAD_W), lambda b, h, i: (b, i, h))
    kv_spec = pl.BlockSpec((1, lp, HEAD_W), lambda b, h, i: (b, 0, h))
    return pl.pallas_call(
        functools.partial(_flash_kernel, lam_init=lam_init, tq=tq),
        out_shape=jax.ShapeDtypeStruct((nb, lp, D_MODEL), bf16),
        grid=(nb, HEADS, nq),
        in_specs=[pl.BlockSpec((4, SUB_W), lambda b, h, i: (0, 0)),
                  pl.BlockSpec((1, HEAD_W), lambda b, h, i: (0, 0)),
                  q_spec, kv_spec, kv_spec],
        out_specs=q_spec,
        scratch_shapes=[pltpu.VMEM((2 * tq, HEAD_W), bf16), pltpu.VMEM((2 * tq, LANES), f32),
                        pltpu.VMEM((2 * tq, LANES), f32), pltpu.VMEM((2 * tq, HEAD_W), f32)],
        compiler_params=_params(("parallel", "parallel", "arbitrary"), 48),
        name="flash_diff_attention",
    )(lam_p, g, qb, kb, vb)


def _decode_kernel(pt_ref, lam_ref, g_ref, q_ref, kn_ref, vn_ref, *rest, lam_init, pages_per_step, n_new):
    k_refs = rest[:pages_per_step]
    v_refs = rest[pages_per_step:2 * pages_per_step]
    o_ref = rest[2 * pages_per_step]
    m_sc, l_sc, acc_sc = rest[2 * pages_per_step + 1:]
    del pt_ref
    p_idx = pl.program_id(1)
    rph = 2 * n_new
    n_rows = HEADS * rph
    page = k_refs[0].shape[1] // HEADS

    @pl.when(p_idx == 0)
    def _():
        m_sc[...] = jnp.full(m_sc.shape, NEG_BIG, f32)
        l_sc[...] = jnp.zeros(l_sc.shape, f32)
        acc_sc[...] = jnp.zeros(acc_sc.shape, f32)

    def head_rows(ref, h):
        return _split_hi_lo(ref[0, pl.ds(h, page, stride=HEADS), :])

    def fold(a):
        return a[0:rph] + a[rph:2 * rph]

    def step(kv_refs, masked):
        scores = []
        for k_ref, _ in kv_refs:
            per_head = []
            for h in range(HEADS):
                q2 = q_ref[0, h]
                k_hi, k_lo = head_rows(k_ref, h)
                a = (lax.dot_general(q2, k_hi, NT_DIMS, preferred_element_type=f32)
                     + lax.dot_general(q2, k_lo, NT_DIMS, preferred_element_type=f32))
                per_head.append(fold(a))
            s = jnp.concatenate(per_head, axis=0)
            if masked:
                r = lax.broadcasted_iota(jnp.int32, (n_rows, page), 0)
                tok = r % n_new
                cidx = lax.broadcasted_iota(jnp.int32, (n_rows, page), 1)
                s = jnp.where(cidx <= tok, s, NEG_BIG)
            scores.append(s)
        m_prev = m_sc[...]
        m_new = m_prev
        for s in scores:
            m_new = jnp.maximum(m_new, jnp.max(s, axis=-1, keepdims=True))
        alpha = jnp.exp2(m_prev - m_new)
        l_new = alpha * l_sc[...]
        pv = [None] * HEADS
        for s, (_, v_ref) in zip(scores, kv_refs):
            p = jnp.exp2(s - m_new)
            l_new = l_new + jnp.sum(p, axis=-1, keepdims=True)
            for h in range(HEADS):
                p2 = jnp.concatenate(_split_hi_lo(p[h * rph:(h + 1) * rph]), axis=0)
                v_hi, v_lo = head_rows(v_ref, h)
                part = fold(jnp.dot(p2, v_hi, preferred_element_type=f32)
                            + jnp.dot(p2, v_lo, preferred_element_type=f32))
                pv[h] = part if pv[h] is None else pv[h] + part
        l_sc[...] = l_new
        for h in range(HEADS):
            rs = slice(h * rph, (h + 1) * rph)
            acc_sc[rs, :] = alpha[rs, :] * acc_sc[rs, :] + pv[h]
        m_sc[...] = m_new

    step(list(zip(k_refs, v_refs)), False)

    @pl.when(p_idx == pl.num_programs(1) - 1)
    def _():
        step([(kn_ref, vn_ref)], True)
        lam = _diff_lambda(lam_ref, lam_init)
        o = acc_sc[...] / l_sc[...]
        for h in range(HEADS):
            o1 = o[h * rph:h * rph + n_new, :]
            o2 = o[h * rph + n_new:(h + 1) * rph, :]
            o_ref[0, :, h * HEAD_W:(h + 1) * HEAD_W] = _diff_finish(o1, o2, lam, g_ref[...], lam_init)


def _decode_attention(q, k_new, v_new, cache_k, cache_v, page_table, page_base, lam_p, g, lam_init,
                      pages_per_step):
    nbd, n_new, _ = q.shape
    rph = 2 * n_new
    assert rph % 8 == 0, "each head's (sub-head, token) rows must fill whole sublane tiles"
    page_rows = cache_k.shape[1]
    n_pages = page_table.shape[1]
    n_rows = HEADS * rph
    qh = q.reshape(nbd, n_new, HEADS, HEAD_W).transpose(0, 2, 1, 3)
    lane_sub = jnp.arange(HEAD_W, dtype=jnp.int32) // SUB_W
    q_sub = [jnp.where(lane_sub == c, qh, jnp.zeros_like(qh)) for c in range(2)]
    q2 = jnp.concatenate(_split_hi_lo(jnp.concatenate(q_sub, axis=2)), axis=2)
    new_rows = lambda a: jnp.pad(a.reshape(nbd, n_new * HEADS, HEAD_W), [(0, 0), (0, page_rows - n_new * HEADS), (0, 0)])
    kn = new_rows(k_new)
    vn = new_rows(v_new)
    steps = n_pages // pages_per_step

    def page_spec(j):
        return pl.BlockSpec((1, page_rows, HEAD_W),
                            lambda b, p, pt: (page_base + pt[b, p * pages_per_step + j], 0, 0))

    per_seq = lambda shape: pl.BlockSpec((1,) + shape, lambda b, p, pt: (b,) + (0,) * len(shape))
    grid_spec = pltpu.PrefetchScalarGridSpec(
        num_scalar_prefetch=1,
        grid=(nbd, steps),
        in_specs=[pl.BlockSpec((4, SUB_W), lambda b, p, pt: (0, 0)),
                  pl.BlockSpec((1, HEAD_W), lambda b, p, pt: (0, 0)),
                  per_seq((HEADS, 2 * rph, HEAD_W)), per_seq((page_rows, HEAD_W)), per_seq((page_rows, HEAD_W))]
                 + [page_spec(j) for j in range(pages_per_step)]
                 + [page_spec(j) for j in range(pages_per_step)],
        out_specs=per_seq((n_new, D_MODEL)),
        scratch_shapes=[pltpu.VMEM((n_rows, 1), f32), pltpu.VMEM((n_rows, 1), f32),
                        pltpu.VMEM((n_rows, HEAD_W), f32)],
    )
    return pl.pallas_call(
        functools.partial(_decode_kernel, lam_init=lam_init, pages_per_step=pages_per_step, n_new=n_new),
        out_shape=jax.ShapeDtypeStruct((nbd, n_new, D_MODEL), f32),
        grid_spec=grid_spec,
        compiler_params=_params(("parallel", "arbitrary"), 40),
        name="paged_decode_attention",
    )(page_table, lam_p, g, q2, kn, vn, *([cache_k] * pages_per_step), *([cache_v] * pages_per_step))


def _proj_ln_kernel(o_ref, x_ref, w_ref, g_ref, b_ref, xo_ref, xb_ref):
    y = _matmul(o_ref[0], w_ref[...])
    out = _layer_norm(DN_ALPHA * x_ref[0] + y, g_ref[...], b_ref[...])
    xo_ref[0] = out
    xb_ref[0] = out.astype(bf16)


def _proj_ln(o, x, w, ln_g, ln_b, tm):
    nb, seq_len, _ = x.shape
    row_spec = pl.BlockSpec((1, tm, D_MODEL), lambda b, i: (b, i, 0))
    vec_spec = pl.BlockSpec((1, D_MODEL), lambda b, i: (0, 0))
    return pl.pallas_call(
        _proj_ln_kernel,
        out_shape=(jax.ShapeDtypeStruct(x.shape, f32), jax.ShapeDtypeStruct(x.shape, bf16)),
        grid=(nb, pl.cdiv(seq_len, tm)),
        in_specs=[row_spec, row_spec, pl.BlockSpec((D_MODEL, D_MODEL), lambda b, i: (0, 0)), vec_spec, vec_spec],
        out_specs=(row_spec, row_spec),
        compiler_params=_params(("parallel", "parallel"), 32),
        name="proj_residual_layernorm",
    )(o, x, w, ln_g, ln_b)


def _router_kernel(x_ref, wh_ref, wl_ref, b_ref, gate_ref):
    xh, xl = _split_hi_lo(x_ref[...])
    wh = wh_ref[...]
    logits = (lax.dot_general(wh, xh, NT_DIMS, preferred_element_type=f32)
              + (lax.dot_general(wh, xl, NT_DIMS, preferred_element_type=f32)
                 + lax.dot_general(wl_ref[...], xh, NT_DIMS, preferred_element_type=f32)))
    s = jax.nn.sigmoid(logits)
    sel = s + b_ref[...]
    rows = [sel[e:e + 1, :] for e in range(N_EXPERTS)]
    scores = []
    for gidx in range(N_GROUPS):
        r = rows[gidx * EXPERTS_PER_GROUP:(gidx + 1) * EXPERTS_PER_GROUP]
        best = None
        for a in range(EXPERTS_PER_GROUP):
            for b in range(a + 1, EXPERTS_PER_GROUP):
                pair = r[a] + r[b]
                best = pair if best is None else jnp.maximum(best, pair)
        scores.append(best)
    picked = []
    for e in range(N_EXPERTS):
        gidx, loc = divmod(e, EXPERTS_PER_GROUP)
        win = None
        for j in range(N_GROUPS):
            if j == gidx:
                continue
            c = scores[gidx] > scores[j] if j < gidx else scores[gidx] >= scores[j]
            win = c if win is None else jnp.logical_and(win, c)
        rank = jnp.zeros_like(rows[e])
        for j in range(EXPERTS_PER_GROUP):
            if j == loc:
                continue
            o = rows[gidx * EXPERTS_PER_GROUP + j]
            ahead = o >= rows[e] if j < loc else o > rows[e]
            rank = rank + ahead.astype(f32)
        chosen = jnp.logical_and(win, rank < 1.5)
        picked.append(jnp.where(chosen, s[e:e + 1, :], 0.0))
    total = picked[0]
    for e in range(1, N_EXPERTS):
        total = total + picked[e]
    for e in range(N_EXPERTS):
        gate_ref[e:e + 1, :] = picked[e] / total


def _router(x, w_hi, w_lo, b_col, tm):
    n_rows = x.shape[0]
    return pl.pallas_call(
        _router_kernel,
        out_shape=jax.ShapeDtypeStruct((N_EXPERTS, n_rows), f32),
        grid=(pl.cdiv(n_rows, tm),),
        in_specs=[pl.BlockSpec((tm, D_MODEL), lambda i: (i, 0)),
                  pl.BlockSpec((N_EXPERTS, D_MODEL), lambda i: (0, 0)),
                  pl.BlockSpec((N_EXPERTS, D_MODEL), lambda i: (0, 0)),
                  pl.BlockSpec((N_EXPERTS, 1), lambda i: (0, 0))],
        out_specs=pl.BlockSpec((N_EXPERTS, tm), lambda i: (0, i)),
        compiler_params=_params(("parallel",), 32),
        name="moe_router",
    )(x, w_hi, w_lo, b_col)


def _moe_kernel(xb_ref, x_ref, gate_ref, wg_ref, wu_ref, wd_ref, g_ref, b_ref, xo_ref, xbo_ref, acc_sc):
    e = pl.program_id(1)

    @pl.when(e == 0)
    def _():
        acc_sc[...] = jnp.zeros(acc_sc.shape, f32)

    x = xb_ref[...]
    a = jnp.dot(x, wg_ref[0].astype(bf16), preferred_element_type=f32)
    u = jnp.dot(x, wu_ref[0].astype(bf16), preferred_element_type=f32)
    hmid = (a * jax.nn.sigmoid(a) * u).astype(bf16)
    y = jnp.dot(hmid, wd_ref[0].astype(bf16), preferred_element_type=f32)
    gate = gate_ref[...]
    lane = lax.broadcasted_iota(jnp.int32, gate.shape, 1)
    gcol = jnp.sum(jnp.where(lane == e, gate, 0.0), axis=-1, keepdims=True)
    acc_sc[...] += gcol * y

    @pl.when(e == N_EXPERTS - 1)
    def _():
        out = _layer_norm(DN_ALPHA * x_ref[...] + acc_sc[...], g_ref[...], b_ref[...])
        xo_ref[...] = out
        xbo_ref[...] = out.astype(bf16)


def _moe_ln(x, x_b16, gate, wg, wu, wd, ln_g, ln_b, tm):
    n_rows = x.shape[0]
    row_spec = pl.BlockSpec((tm, D_MODEL), lambda i, e: (i, 0))
    vec_spec = pl.BlockSpec((1, D_MODEL), lambda i, e: (0, 0))
    return pl.pallas_call(
        _moe_kernel,
        out_shape=(jax.ShapeDtypeStruct(x.shape, f32), jax.ShapeDtypeStruct(x.shape, bf16)),
        grid=(pl.cdiv(n_rows, tm), N_EXPERTS),
        in_specs=[row_spec, row_spec, pl.BlockSpec((tm, N_EXPERTS), lambda i, e: (i, 0)),
                  pl.BlockSpec((1, D_MODEL, D_EXPERT), lambda i, e: (e, 0, 0)),
                  pl.BlockSpec((1, D_MODEL, D_EXPERT), lambda i, e: (e, 0, 0)),
                  pl.BlockSpec((1, D_EXPERT, D_MODEL), lambda i, e: (e, 0, 0)),
                  vec_spec, vec_spec],
        out_specs=(row_spec, row_spec),
        scratch_shapes=[pltpu.VMEM((tm, D_MODEL), f32)],
        compiler_params=_params(("parallel", "arbitrary"), 56),
        name="moe_experts_layernorm",
    )(x_b16, x, gate, wg, wu, wd, ln_g, ln_b)


def _moe_block(x, x_b16, router_w, wg, wu, wd, ln_g, ln_b, tm_router, tm_moe):
    shp = x.shape
    xf = x.reshape(-1, D_MODEL)
    xbf = x_b16.reshape(-1, D_MODEL)
    gate_t = _router(xf, *router_w, tm_router)
    xo, xbo = _moe_ln(xf, xbf, gate_t.T, wg, wu, wd, ln_g, ln_b, tm_moe)
    return xo.reshape(shp), xbo.reshape(shp)


def _mlstm_proj_kernel(x_ref, w_ref, wgh_ref, wgl_ref, wgth_ref, wgtl_ref, z_ref, gcol_ref, grow_ref, *, seq_len, tm):
    i = pl.program_id(1)
    rows = i * tm + lax.broadcasted_iota(jnp.int32, (tm, 1), 0)
    x = x_ref[0]
    x = jnp.where(rows < seq_len, x, jnp.zeros_like(x))
    for part in range(3):
        sl = slice(part * D_MODEL, (part + 1) * D_MODEL)
        z_ref[0, :, sl] = jnp.dot(x, w_ref[:, sl], preferred_element_type=f32).astype(bf16)
    gcol_ref[0] = (jnp.dot(x, wgh_ref[...], preferred_element_type=f32)
                   + jnp.dot(x, wgl_ref[...], preferred_element_type=f32))
    grow_ref[0] = (lax.dot_general(wgth_ref[...], x, NT_DIMS, preferred_element_type=f32)
                   + lax.dot_general(wgtl_ref[...], x, NT_DIMS, preferred_element_type=f32))


def _mlstm_proj(x_b16, w_main, wg_cols, wg_rows, tm):
    nb, seq_len, _ = x_b16.shape
    nt = pl.cdiv(seq_len, tm)
    lp = nt * tm
    const = lambda shape: pl.BlockSpec(shape, lambda b, i: (0,) * len(shape))
    return pl.pallas_call(
        functools.partial(_mlstm_proj_kernel, seq_len=seq_len, tm=tm),
        out_shape=(jax.ShapeDtypeStruct((nb, lp, 3 * D_MODEL), bf16),
                   jax.ShapeDtypeStruct((nb, lp, LANES), f32),
                   jax.ShapeDtypeStruct((nb, 2 * HEADS, lp), f32)),
        grid=(nb, nt),
        in_specs=[pl.BlockSpec((1, tm, D_MODEL), lambda b, i: (b, i, 0)),
                  const((D_MODEL, 3 * D_MODEL)), const((D_MODEL, LANES)), const((D_MODEL, LANES)),
                  const((2 * HEADS, D_MODEL)), const((2 * HEADS, D_MODEL))],
        out_specs=(pl.BlockSpec((1, tm, 3 * D_MODEL), lambda b, i: (b, i, 0)),
                   pl.BlockSpec((1, tm, LANES), lambda b, i: (b, i, 0)),
                   pl.BlockSpec((1, 2 * HEADS, tm), lambda b, i: (b, 0, i))),
        compiler_params=_params(("parallel", "parallel"), 48),
        name="mlstm_in_proj",
    )(x_b16, w_main, *wg_cols, *wg_rows)


def _softcap(z):
    return GATE_SOFTCAP * jnp.tanh(z / GATE_SOFTCAP)


def _log_sigmoid(z):
    return jnp.minimum(z, 0.0) - jnp.log(1.0 + jnp.exp(-jnp.abs(z)))


def _mlstm_chunk_kernel(z_ref, gcol_ref, grow_ref, bcol_ref, brow_ref, g_ref, c0_ref, n0_ref, m0_ref,
                        h_ref, c_ref, n_ref, m_ref, c_sc, n_sc, m_sc, *, seq_len, tc):
    ci = pl.program_id(1)

    @pl.when(ci == 0)
    def _():
        c_sc[...] = c0_ref[0]
        n_sc[...] = n0_ref[0]
        m_sc[...] = m0_ref[0]

    t_col = ci * tc + lax.broadcasted_iota(jnp.int32, (tc, LANES), 0)
    lane = lax.broadcasted_iota(jnp.int32, (tc, LANES), 1)
    pre = _softcap(gcol_ref[0] + bcol_ref[...])
    gates_c = jnp.where(lane < HEADS, pre, _log_sigmoid(pre))
    neutral_c = jnp.where(lane < HEADS, NEG_BIG, 0.0)
    gates_c = jnp.where(t_col < seq_len, gates_c, neutral_c)
    t_row = ci * tc + lax.broadcasted_iota(jnp.int32, (2 * HEADS, tc), 1)
    sub = lax.broadcasted_iota(jnp.int32, (2 * HEADS, tc), 0)
    pre_r = _softcap(grow_ref[0] + brow_ref[...])
    gates_r = jnp.where(sub < HEADS, pre_r, _log_sigmoid(pre_r))
    neutral_r = jnp.where(sub < HEADS, NEG_BIG, 0.0)
    gates_r = jnp.where(t_row < seq_len, gates_r, neutral_r)

    ri = lax.broadcasted_iota(jnp.int32, (tc, tc), 0)
    cj = lax.broadcasted_iota(jnp.int32, (tc, tc), 1)
    causal = cj <= ri
    tril = jnp.where(causal, 1.0, 0.0).astype(bf16)
    triu = jnp.where(ri <= cj, 1.0, 0.0).astype(bf16)
    hi, lo = _split_hi_lo(gates_c)
    cum_c = jnp.dot(tril, hi, preferred_element_type=f32) + jnp.dot(tril, lo, preferred_element_type=f32)
    hi, lo = _split_hi_lo(gates_r)
    cum_r = jnp.dot(hi, triu, preferred_element_type=f32) + jnp.dot(lo, triu, preferred_element_type=f32)

    k_scale = M_DQK ** -0.5
    for h in range(HEADS):
        q = z_ref[0, :, h * M_DQK:(h + 1) * M_DQK]
        k = z_ref[0, :, HEADS * M_DQK + h * M_DQK:HEADS * M_DQK + (h + 1) * M_DQK]
        v = z_ref[0, :, D_MODEL + h * M_DV:D_MODEL + (h + 1) * M_DV]
        og = z_ref[0, :, 2 * D_MODEL + h * M_DV:2 * D_MODEL + (h + 1) * M_DV]
        b_col = cum_c[:, HEADS + h:HEADS + h + 1]
        a_col = gates_c[:, h:h + 1] - b_col
        a_row = gates_r[h:h + 1, :] - cum_r[HEADS + h:HEADS + h + 1, :]
        m0 = m_sc[h:h + 1, 0:1]
        c0 = c_sc[h]
        n0 = n_sc[h:h + 1, :]

        amat = jnp.where(causal, a_row, NEG_BIG)
        u = jnp.maximum(m0, jnp.max(amat, axis=-1, keepdims=True))
        dmat = jnp.exp(amat - u)
        s = lax.dot_general(q, k, NT_DIMS, preferred_element_type=f32) * k_scale
        w = s * dmat
        g_in = jnp.exp(m0 - u)
        num = (jnp.dot(w.astype(bf16), v, preferred_element_type=f32)
               + g_in * jnp.dot(q, c0.astype(bf16), preferred_element_type=f32))
        qn = jnp.sum(q.astype(f32) * n0, axis=-1, keepdims=True)
        den = jnp.sum(w, axis=-1, keepdims=True) + g_in * qn
        m_tok = b_col + u
        hval = num / jnp.maximum(jnp.abs(den), jnp.exp(-m_tok))
        hval = hval * lax.rsqrt(jnp.mean(hval * hval, axis=-1, keepdims=True) + RMS_EPS)
        hval = hval * g_ref[:, h * M_DV:(h + 1) * M_DV] * jax.nn.sigmoid(og.astype(f32))
        h_ref[0, :, h * M_DV:(h + 1) * M_DV] = hval.astype(bf16)

        u_last = u[tc - 1:tc, :]
        g0 = jnp.exp(m0 - u_last)
        decay = jnp.exp(a_col - u_last)
        kd = k.astype(f32) * (decay * k_scale)
        c_sc[h] = g0 * c0 + lax.dot_general(kd.astype(bf16), v, TN_DIMS, preferred_element_type=f32)
        n_sc[h:h + 1, :] = g0 * n0 + jnp.sum(kd, axis=0, keepdims=True)
        m_sc[h:h + 1, :] = jnp.broadcast_to(b_col[tc - 1:tc, :] + u_last, (1, LANES))

    @pl.when(ci == pl.num_programs(1) - 1)
    def _():
        c_ref[0] = c_sc[...]
        n_ref[0] = n_sc[...]
        m_ref[0] = m_sc[...]


def _mlstm_chunks(z, gcol, grow, bcol, brow, norm_g, c0t, n0, m0b, seq_len, tc):
    nb, lp, _ = z.shape
    nc = lp // tc
    per_b = lambda shape: pl.BlockSpec((1,) + shape, lambda b, c: (b,) + (0,) * len(shape))
    const = lambda shape: pl.BlockSpec(shape, lambda b, c: (0,) * len(shape))
    state_specs = (per_b((HEADS, M_DQK, M_DV)), per_b((HEADS, M_DQK)), per_b((HEADS, LANES)))
    return pl.pallas_call(
        functools.partial(_mlstm_chunk_kernel, seq_len=seq_len, tc=tc),
        out_shape=(jax.ShapeDtypeStruct((nb, lp, D_MODEL), bf16),
                   jax.ShapeDtypeStruct((nb, HEADS, M_DQK, M_DV), f32),
                   jax.ShapeDtypeStruct((nb, HEADS, M_DQK), f32),
                   jax.ShapeDtypeStruct((nb, HEADS, LANES), f32)),
        grid=(nb, nc),
        in_specs=[pl.BlockSpec((1, tc, 3 * D_MODEL), lambda b, c: (b, c, 0)),
                  pl.BlockSpec((1, tc, LANES), lambda b, c: (b, c, 0)),
                  pl.BlockSpec((1, 2 * HEADS, tc), lambda b, c: (b, 0, c)),
                  const((1, LANES)), const((2 * HEADS, 1)), const((1, D_MODEL))] + list(state_specs),
        out_specs=(pl.BlockSpec((1, tc, D_MODEL), lambda b, c: (b, c, 0)),) + state_specs,
        scratch_shapes=[pltpu.VMEM((HEADS, M_DQK, M_DV), f32), pltpu.VMEM((HEADS, M_DQK), f32),
                        pltpu.VMEM((HEADS, LANES), f32)],
        compiler_params=_params(("parallel", "arbitrary"), 32),
        name="mlstm_chunks",
    )(z, gcol, grow, bcol, brow, norm_g, c0t, n0, m0b)


ATT_TILE = 512
SAMPLE_TILE = 128
MLSTM_CHUNK = 128
PAGES_PER_STEP = 4
MOE_TILE = 1024
ROUTER_TILE = 512


def _attention_layer(xp, xp_b, xs, cache_k, cache_v, layer, page_table, w_in, lam_p, g, w_out, ln_g, ln_b, lam_init):
    nbd, n_new, _ = xs.shape
    seq_len = xp.shape[1]
    past = page_table.shape[1] * cache_k.shape[2]
    qb, kb, vb, kf, vf = _qkv_rope(xp_b, w_in.astype(bf16), jnp.arange(seq_len, dtype=jnp.int32), ATT_TILE)
    o = _flash_attention(qb, kb, vb, lam_p, g, lam_init, ATT_TILE)
    xp1, xp1_b = _proj_ln(o, xp, w_out.astype(bf16), ln_g, ln_b, ATT_TILE)
    pos_s = past + (jnp.arange(nbd * n_new, dtype=jnp.int32) % n_new)
    xs_rows = xs.reshape(1, nbd * n_new, D_MODEL)
    _, _, _, ksf, vsf, qs = _qkv_rope(xs_rows, w_in, pos_s, SAMPLE_TILE, q_f32=True)
    qs = qs.reshape(nbd, n_new, D_MODEL)
    ksf = ksf.reshape(nbd, n_new, D_MODEL)
    vsf = vsf.reshape(nbd, n_new, D_MODEL)
    n_pool, page = cache_k.shape[1], cache_k.shape[2]
    ck = cache_k.reshape(-1, page * HEADS, HEAD_W)
    cv = cache_v.reshape(-1, page * HEADS, HEAD_W)
    os_ = _decode_attention(qs, ksf, vsf, ck, cv, page_table, layer * n_pool, lam_p, g, lam_init, PAGES_PER_STEP)
    os_rows = os_.reshape(1, nbd * n_new, D_MODEL)
    xs1, xs1_b = _proj_ln(os_rows, xs_rows, w_out, ln_g, ln_b, SAMPLE_TILE)
    return (xp1, xp1_b, xs1.reshape(xs.shape), xs1_b.reshape(xs.shape),
            kf.reshape(kf.shape[:2] + (HEADS, HEAD_W)), vf.reshape(vf.shape[:2] + (HEADS, HEAD_W)),
            ksf.reshape(nbd, n_new, HEADS, HEAD_W), vsf.reshape(nbd, n_new, HEADS, HEAD_W))


def _mlstm_stream(x, x_b, w_main, wg_cols, wg_rows, bcol, brow, norm_g, c0t, n0, m0b, w_out, ln_g, ln_b):
    nb, seq_len, _ = x.shape
    short = seq_len < MLSTM_CHUNK
    if short:
        rows = nb * seq_len
        z, gcol, grow = _mlstm_proj(x_b.reshape(1, rows, D_MODEL), w_main, wg_cols, wg_rows, rows)
        pad = MLSTM_CHUNK - seq_len
        z = jnp.pad(z[0, :rows].reshape(nb, seq_len, 3 * D_MODEL), [(0, 0), (0, pad), (0, 0)])
        gcol = jnp.pad(gcol[0, :rows].reshape(nb, seq_len, LANES), [(0, 0), (0, pad), (0, 0)])
        grow = jnp.transpose(grow[0, :, :rows].reshape(2 * HEADS, nb, seq_len), (1, 0, 2))
        grow = jnp.pad(grow, [(0, 0), (0, 0), (0, pad)])
    else:
        z, gcol, grow = _mlstm_proj(x_b, w_main, wg_cols, wg_rows, MLSTM_CHUNK)
    hb, ct, n, mb = _mlstm_chunks(z, gcol, grow, bcol, brow, norm_g, c0t, n0, m0b, seq_len, MLSTM_CHUNK)
    if short:
        hb = hb[:, :seq_len].reshape(1, rows, D_MODEL)
        x1, x1_b = _proj_ln(hb, x.reshape(1, rows, D_MODEL), w_out, ln_g, ln_b, rows)
        x1, x1_b = x1.reshape(x.shape), x1_b.reshape(x.shape)
    else:
        x1, x1_b = _proj_ln(hb, x, w_out, ln_g, ln_b, ATT_TILE)
    return x1, x1_b, jnp.swapaxes(ct, -1, -2), n, mb[..., 0]


def kernel(x_prompt, x_sample, cache_k, cache_v, state_C, state_n, state_m, page_table, meta_tokens, w_attn_in, lambda_q1, lambda_k1, lambda_q2, lambda_k2, subln_g, w_attn_out, w_mlstm_in, b_mlstm_if, mlstm_norm_g, w_mlstm_out, w_router, b_router, w_exp_gate, w_exp_up, w_exp_down, ln_g, ln_b):
    nb = x_prompt.shape[0]
    nbd = x_sample.shape[0]
    meta = jnp.broadcast_to(meta_tokens.astype(x_prompt.dtype)[None], (nb, N_META, D_MODEL))
    xp = jnp.concatenate([meta, x_prompt], axis=1)
    xs = x_sample
    xp_b = xp.astype(bf16)
    router_w = _split_hi_lo(w_router.T) + (b_router.astype(f32).reshape(N_EXPERTS, 1),)
    prompt_moe_tile = min(MOE_TILE, xp.shape[0] * xp.shape[1])
    sample_rows = nbd * xs.shape[1]

    def moe_pair(i, xp, xp_b, xs, xs_b):
        args = (router_w, w_exp_gate[i], w_exp_up[i], w_exp_down[i], ln_g[i, 1:2], ln_b[i, 1:2])
        xp, xp_b = _moe_block(xp, xp_b, *args, ROUTER_TILE, prompt_moe_tile)
        xs, xs_b = _moe_block(xs, xs_b, *args, sample_rows, sample_rows)
        return xp, xp_b, xs, xs_b

    lam_init = 0.8 - 0.6 * math.exp(-0.3 * 0)
    lam_p = jnp.stack([lambda_q1[0], lambda_k1[0], lambda_q2[0], lambda_k2[0]]).astype(f32)
    (xp, xp_b, xs, xs_b, k_p, v_p, k_s, v_s) = _attention_layer(
        xp, xp_b, xs, cache_k, cache_v, 0, page_table, w_attn_in[0], lam_p,
        subln_g[0].reshape(1, HEAD_W), w_attn_out[0], ln_g[0, 0:1], ln_b[0, 0:1], lam_init)
    xp, xp_b, xs, xs_b = moe_pair(0, xp, xp_b, xs, xs_b)

    w_in = w_mlstm_in[0]
    w_main = w_in[:, :3 * D_MODEL].astype(bf16)
    w_gate = w_in[:, 3 * D_MODEL:]
    wg_cols = _split_hi_lo(jnp.pad(w_gate, [(0, 0), (0, LANES - 2 * HEADS)]))
    wg_rows = _split_hi_lo(w_gate.T)
    b_if = b_mlstm_if[0].astype(f32).reshape(2 * HEADS)
    bcol = jnp.pad(b_if, (0, LANES - 2 * HEADS)).reshape(1, LANES)
    brow = b_if.reshape(2 * HEADS, 1)
    norm_g = mlstm_norm_g[0].reshape(1, D_MODEL)
    common = (w_main, wg_cols, wg_rows, bcol, brow, norm_g)
    tail = (w_mlstm_out[0].astype(bf16), ln_g[1, 0:1], ln_b[1, 0:1])
    zero_state = (jnp.zeros((nb, HEADS, M_DQK, M_DV), f32), jnp.zeros((nb, HEADS, M_DQK), f32),
                  jnp.zeros((nb, HEADS, LANES), f32))
    xp, xp_b, c_p, n_p, m_p = _mlstm_stream(xp, xp_b, *common, *zero_state, *tail)
    sample_state = (jnp.swapaxes(state_C[0].astype(f32), -1, -2), state_n[0].astype(f32),
                    jnp.broadcast_to(state_m[0].astype(f32)[..., None], (nbd, HEADS, LANES)))
    xs, xs_b, c_s, n_s, m_s = _mlstm_stream(xs, xs_b, *common, *sample_state, *tail)
    xp, xp_b, xs, xs_b = moe_pair(1, xp, xp_b, xs, xs_b)

    return (xp[:, N_META:], xs, k_p[None], v_p[None], k_s[None], v_s[None],
            c_p[None], n_p[None], m_p[None], c_s[None], n_s[None], m_s[None])
```

```python
import functools
import math

import jax
import jax.numpy as jnp
from jax import lax
from jax.experimental import pallas as pl
from jax.experimental.pallas import tpu as pltpu

f32 = jnp.float32
bf16 = jnp.bfloat16

D_MODEL = 1024
N_META = 16
HEADS = 8
HEAD_W = 128
SUB_W = 64
ATT_SCALE = SUB_W ** -0.5
Q_SCALE = ATT_SCALE * math.log2(math.e)
ROT_DIM = 16
ROPE_THETA = 500000.0
M_DQK = 64
M_DV = 128
GATE_SOFTCAP = 15.0
N_EXPERTS = 16
N_GROUPS = 4
EXPERTS_PER_GROUP = 4
D_EXPERT = 512
DEPTH = 2
DN_ALPHA = (2 * DEPTH) ** 0.25
LN_EPS = 1e-5
RMS_EPS = 1e-6
NEG_BIG = -1e30

LANES = 128
MIB = 1024 * 1024

NT_DIMS = (((1,), (1,)), ((), ()))
TN_DIMS = (((0,), (0,)), ((), ()))


def _params(semantics, vmem_mib):
    return pltpu.CompilerParams(dimension_semantics=semantics, vmem_limit_bytes=vmem_mib * MIB)


def _split_hi_lo(a):
    hi = a.astype(bf16)
    lo = (a - hi.astype(f32)).astype(bf16)
    return hi, lo


def _matmul(x, w):
    if x.dtype == bf16:
        return jnp.dot(x, w, preferred_element_type=f32)
    xh, xl = _split_hi_lo(x)
    wh, wl = _split_hi_lo(w)
    return (jnp.dot(xh, wh, preferred_element_type=f32)
            + (jnp.dot(xl, wh, preferred_element_type=f32) + jnp.dot(xh, wl, preferred_element_type=f32)))


def _layer_norm(r, g, b):
    mu = jnp.mean(r, axis=-1, keepdims=True)
    rc = r - mu
    var = jnp.mean(rc * rc, axis=-1, keepdims=True)
    return rc * lax.rsqrt(var + LN_EPS) * g + b


def _qkv_rope_kernel(x_ref, w_ref, c_ref, s1_ref, s2_ref,
                     qb_ref, kb_ref, vb_ref, kf_ref, vf_ref, *maybe_qf_ref, seq_len, tm):
    i = pl.program_id(1)
    rows = i * tm + lax.broadcasted_iota(jnp.int32, (tm, 1), 0)
    valid = rows < seq_len
    x = x_ref[0]
    c = c_ref[...]
    s1 = s1_ref[...]
    s2 = s2_ref[...]
    for part in range(3):
        y = _matmul(x, w_ref[:, part * D_MODEL:(part + 1) * D_MODEL])
        for h in range(HEADS):
            sl = slice(h * HEAD_W, (h + 1) * HEAD_W)
            yh = y[:, sl]
            if part < 2:
                yh = yh * c + pltpu.roll(yh, LANES - ROT_DIM // 2, 1) * s1 + pltpu.roll(yh, ROT_DIM // 2, 1) * s2
            if part == 0:
                qb_ref[0, :, sl] = jnp.where(valid, yh * Q_SCALE, 0.0).astype(bf16)
                for qf_ref in maybe_qf_ref:
                    qf_ref[0, :, sl] = yh * Q_SCALE
            elif part == 1:
                kf_ref[0, :, sl] = yh
                kb_ref[0, :, sl] = jnp.where(valid, yh, 0.0).astype(bf16)
            else:
                vf_ref[0, :, sl] = yh
                vb_ref[0, :, sl] = jnp.where(valid, yh, 0.0).astype(bf16)


def _rope_tables(pos):
    half = ROT_DIM // 2
    inv_freq = ROPE_THETA ** (-jnp.arange(half, dtype=f32) * (2.0 / ROT_DIM))
    ang = pos.astype(f32)[:, None] * inv_freq[None, :]
    cos = jnp.cos(ang)
    sin = jnp.sin(ang)
    t = pos.shape[0]
    ones = jnp.ones((t, SUB_W - ROT_DIM), f32)
    zeros8 = jnp.zeros((t, half), f32)
    zeros48 = jnp.zeros((t, SUB_W - ROT_DIM), f32)
    c = jnp.concatenate([cos, cos, ones], axis=1)
    s1 = jnp.concatenate([-sin, zeros8, zeros48], axis=1)
    s2 = jnp.concatenate([zeros8, sin, zeros48], axis=1)
    return tuple(jnp.concatenate([a, a], axis=1) for a in (c, s1, s2))


def _qkv_rope(x, w, pos, tm, q_f32=False):
    nb, seq_len, _ = x.shape
    nt = pl.cdiv(seq_len, tm)
    lp = nt * tm
    c, s1, s2 = _rope_tables(pos)
    row_spec = pl.BlockSpec((1, tm, D_MODEL), lambda b, i: (b, i, 0))
    tab_spec = pl.BlockSpec((tm, HEAD_W), lambda b, i: (i, 0))
    pad_shape = jax.ShapeDtypeStruct((nb, lp, D_MODEL), bf16)
    out_shape = jax.ShapeDtypeStruct((nb, seq_len, D_MODEL), f32)
    n_out = 6 if q_f32 else 5
    return pl.pallas_call(
        functools.partial(_qkv_rope_kernel, seq_len=seq_len, tm=tm),
        out_shape=(pad_shape, pad_shape, pad_shape) + (out_shape,) * (n_out - 3),
        grid=(nb, nt),
        in_specs=[row_spec, pl.BlockSpec((D_MODEL, 3 * D_MODEL), lambda b, i: (0, 0)), tab_spec, tab_spec, tab_spec],
        out_specs=(row_spec,) * n_out,
        compiler_params=_params(("parallel", "parallel"), 48),
        name="qkv_rope",
    )(x, w, c, s1, s2)


def _diff_lambda(lam_ref, lam_init):
    p = lam_ref[...]
    l1 = jnp.sum(p[0:1] * p[1:2], axis=-1, keepdims=True)
    l2 = jnp.sum(p[2:3] * p[3:4], axis=-1, keepdims=True)
    return jnp.exp(l1) - jnp.exp(l2) + lam_init


def _diff_finish(o1, o2, lam, g, lam_init):
    o = o1 - lam * o2
    o = o * lax.rsqrt(jnp.mean(o * o, axis=-1, keepdims=True) + RMS_EPS)
    return o * g * (1.0 - lam_init)


def _flash_kernel(lam_ref, g_ref, q_ref, k_ref, v_ref, o_ref, qq_sc, m_sc, l_sc, acc_sc, *, lam_init, tq):
    qi = pl.program_id(2)
    q = q_ref[0]
    lane = lax.broadcasted_iota(jnp.int32, (tq, HEAD_W), 1)
    zero = jnp.zeros_like(q)
    qq_sc[0:tq, :] = jnp.where(lane < SUB_W, q, zero)
    qq_sc[tq:2 * tq, :] = jnp.where(lane >= SUB_W, q, zero)
    m_sc[...] = jnp.full(m_sc.shape, NEG_BIG, f32)
    l_sc[...] = jnp.zeros(l_sc.shape, f32)
    acc_sc[...] = jnp.zeros(acc_sc.shape, f32)

    def step(kstart, masked):
        k = k_ref[0, pl.ds(kstart, tq), :]
        v = v_ref[0, pl.ds(kstart, tq), :]
        s = lax.dot_general(qq_sc[...], k, NT_DIMS, preferred_element_type=f32)
        if masked:
            r = lax.broadcasted_iota(jnp.int32, (2 * tq, tq), 0)
            r = jnp.where(r >= tq, r - tq, r)
            cidx = lax.broadcasted_iota(jnp.int32, (2 * tq, tq), 1)
            s = jnp.where(cidx <= r, s, NEG_BIG)
        m_prev = m_sc[...]
        m_new = jnp.maximum(m_prev, jnp.max(s, axis=-1, keepdims=True))
        alpha = jnp.exp2(m_prev - m_new)
        parts = [jnp.exp2(s[:, j * LANES:(j + 1) * LANES] - m_new) for j in range(tq // LANES)]
        psum = parts[0]
        for part in parts[1:]:
            psum = psum + part
        p = jnp.concatenate([part.astype(bf16) for part in parts], axis=1)
        l_sc[...] = alpha * l_sc[...] + psum
        acc_sc[...] = alpha * acc_sc[...] + jnp.dot(p, v, preferred_element_type=f32)
        m_sc[...] = m_new

    def body(ki, carry):
        step(pl.multiple_of(ki * tq, tq), False)
        return carry

    lax.fori_loop(0, qi, body, 0)
    step(pl.multiple_of(qi * tq, tq), True)

    lam = _diff_lambda(lam_ref, lam_init)
    l = jnp.sum(l_sc[...], axis=-1, keepdims=True)
    o1 = acc_sc[0:tq, :] / l[0:tq, :]
    o2 = acc_sc[tq:2 * tq, :] / l[tq:2 * tq, :]
    o_ref[0] = _diff_finish(o1, o2, lam, g_ref[...], lam_init).astype(bf16)


def _flash_attention(qb, kb, vb, lam_p, g, lam_init, tq):
    nb, lp, _ = qb.shape
    nq = lp // tq
    q_spec = pl.BlockSpec((1, tq, HEAD_W), lambda b, h, i: (b, i, h))
    kv_spec = pl.BlockSpec((1, lp, HEAD_W), lambda b, h, i: (b, 0, h))
    return pl.pallas_call(
        functools.partial(_flash_kernel, lam_init=lam_init, tq=tq),
        out_shape=jax.ShapeDtypeStruct((nb, lp, D_MODEL), bf16),
        grid=(nb, HEADS, nq),
        in_specs=[pl.BlockSpec((4, SUB_W), lambda b, h, i: (0, 0)),
                  pl.BlockSpec((1, HEAD_W), lambda b, h, i: (0, 0)),
                  q_spec, kv_spec, kv_spec],
        out_specs=q_spec,
        scratch_shapes=[pltpu.VMEM((2 * tq, HEAD_W), bf16), pltpu.VMEM((2 * tq, LANES), f32),
                        pltpu.VMEM((2 * tq, LANES), f32), pltpu.VMEM((2 * tq, HEAD_W), f32)],
        compiler_params=_params(("parallel", "parallel", "arbitrary"), 48),
        name="flash_diff_attention",
    )(lam_p, g, qb, kb, vb)


def _decode_kernel(pt_ref, lam_ref, g_ref, q_ref, kn_ref, vn_ref, *rest, lam_init, pages_per_step, n_new):
    k_refs = rest[:pages_per_step]
    v_refs = rest[pages_per_step:2 * pages_per_step]
    o_ref = rest[2 * pages_per_step]
    m_sc, l_sc, acc_sc = rest[2 * pages_per_step + 1:]
    del pt_ref
    p_idx = pl.program_id(1)
    rph = 2 * n_new
    n_rows = HEADS * rph
    page = k_refs[0].shape[1] // HEADS

    @pl.when(p_idx == 0)
    def _():
        m_sc[...] = jnp.full(m_sc.shape, NEG_BIG, f32)
        l_sc[...] = jnp.zeros(l_sc.shape, f32)
        acc_sc[...] = jnp.zeros(acc_sc.shape, f32)

    def head_rows(ref, h):
        return _split_hi_lo(ref[0, pl.ds(h, page, stride=HEADS), :])

    def fold(a):
        return a[0:rph] + a[rph:2 * rph]

    def step(kv_refs, masked):
        scores = []
        for k_ref, _ in kv_refs:
            per_head = []
            for h in range(HEADS):
                q2 = q_ref[0, h]
                k_hi, k_lo = head_rows(k_ref, h)
                a = (lax.dot_general(q2, k_hi, NT_DIMS, preferred_element_type=f32)
                     + lax.dot_general(q2, k_lo, NT_DIMS, preferred_element_type=f32))
                per_head.append(fold(a))
            s = jnp.concatenate(per_head, axis=0)
            if masked:
                r = lax.broadcasted_iota(jnp.int32, (n_rows, page), 0)
                tok = r % n_new
                cidx = lax.broadcasted_iota(jnp.int32, (n_rows, page), 1)
                s = jnp.where(cidx <= tok, s, NEG_BIG)
            scores.append(s)
        m_prev = m_sc[...]
        m_new = m_prev
        for s in scores:
            m_new = jnp.maximum(m_new, jnp.max(s, axis=-1, keepdims=True))
        alpha = jnp.exp2(m_prev - m_new)
        l_new = alpha * l_sc[...]
        pv = [None] * HEADS
        for s, (_, v_ref) in zip(scores, kv_refs):
            p = jnp.exp2(s - m_new)
            l_new = l_new + jnp.sum(p, axis=-1, keepdims=True)
            for h in range(HEADS):
                p2 = jnp.concatenate(_split_hi_lo(p[h * rph:(h + 1) * rph]), axis=0)
                v_hi, v_lo = head_rows(v_ref, h)
                part = fold(jnp.dot(p2, v_hi, preferred_element_type=f32)
                            + jnp.dot(p2, v_lo, preferred_element_type=f32))
                pv[h] = part if pv[h] is None else pv[h] + part
        l_sc[...] = l_new
        for h in range(HEADS):
            rs = slice(h * rph, (h + 1) * rph)
            acc_sc[rs, :] = alpha[rs, :] * acc_sc[rs, :] + pv[h]
        m_sc[...] = m_new

    step(list(zip(k_refs, v_refs)), False)

    @pl.when(p_idx == pl.num_programs(1) - 1)
    def _():
        step([(kn_ref, vn_ref)], True)
        lam = _diff_lambda(lam_ref, lam_init)
        o = acc_sc[...] / l_sc[...]
        for h in range(HEADS):
            o1 = o[h * rph:h * rph + n_new, :]
            o2 = o[h * rph + n_new:(h + 1) * rph, :]
            o_ref[0, :, h * HEAD_W:(h + 1) * HEAD_W] = _diff_finish(o1, o2, lam, g_ref[...], lam_init)


def _decode_attention(q, k_new, v_new, cache_k, cache_v, page_table, page_base, lam_p, g, lam_init,
                      pages_per_step):
    nbd, n_new, _ = q.shape
    rph = 2 * n_new
    assert rph % 8 == 0, "each head's (sub-head, token) rows must fill whole sublane tiles"
    page_rows = cache_k.shape[1]
    n_pages = page_table.shape[1]
    n_rows = HEADS * rph
    qh = q.reshape(nbd, n_new, HEADS, HEAD_W).transpose(0, 2, 1, 3)
    lane_sub = jnp.arange(HEAD_W, dtype=jnp.int32) // SUB_W
    q_sub = [jnp.where(lane_sub == c, qh, jnp.zeros_like(qh)) for c in range(2)]
    q2 = jnp.concatenate(_split_hi_lo(jnp.concatenate(q_sub, axis=2)), axis=2)
    new_rows = lambda a: jnp.pad(a.reshape(nbd, n_new * HEADS, HEAD_W), [(0, 0), (0, page_rows - n_new * HEADS), (0, 0)])
    kn = new_rows(k_new)
    vn = new_rows(v_new)
    steps = n_pages // pages_per_step

    def page_spec(j):
        return pl.BlockSpec((1, page_rows, HEAD_W),
                            lambda b, p, pt: (page_base + pt[b, p * pages_per_step + j], 0, 0))

    per_seq = lambda shape: pl.BlockSpec((1,) + shape, lambda b, p, pt: (b,) + (0,) * len(shape))
    grid_spec = pltpu.PrefetchScalarGridSpec(
        num_scalar_prefetch=1,
        grid=(nbd, steps),
        in_specs=[pl.BlockSpec((4, SUB_W), lambda b, p, pt: (0, 0)),
                  pl.BlockSpec((1, HEAD_W), lambda b, p, pt: (0, 0)),
                  per_seq((HEADS, 2 * rph, HEAD_W)), per_seq((page_rows, HEAD_W)), per_seq((page_rows, HEAD_W))]
                 + [page_spec(j) for j in range(pages_per_step)]
                 + [page_spec(j) for j in range(pages_per_step)],
        out_specs=per_seq((n_new, D_MODEL)),
        scratch_shapes=[pltpu.VMEM((n_rows, 1), f32), pltpu.VMEM((n_rows, 1), f32),
                        pltpu.VMEM((n_rows, HEAD_W), f32)],
    )
    return pl.pallas_call(
        functools.partial(_decode_kernel, lam_init=lam_init, pages_per_step=pages_per_step, n_new=n_new),
        out_shape=jax.ShapeDtypeStruct((nbd, n_new, D_MODEL), f32),
        grid_spec=grid_spec,
        compiler_params=_params(("parallel", "arbitrary"), 40),
        name="paged_decode_attention",
    )(page_table, lam_p, g, q2, kn, vn, *([cache_k] * pages_per_step), *([cache_v] * pages_per_step))


def _proj_ln_kernel(o_ref, x_ref, w_ref, g_ref, b_ref, xo_ref, xb_ref):
    y = _matmul(o_ref[0], w_ref[...])
    out = _layer_norm(DN_ALPHA * x_ref[0] + y, g_ref[...], b_ref[...])
    xo_ref[0] = out
    xb_ref[0] = out.astype(bf16)


def _proj_ln(o, x, w, ln_g, ln_b, tm):
    nb, seq_len, _ = x.shape
    row_spec = pl.BlockSpec((1, tm, D_MODEL), lambda b, i: (b, i, 0))
    vec_spec = pl.BlockSpec((1, D_MODEL), lambda b, i: (0, 0))
    return pl.pallas_call(
        _proj_ln_kernel,
        out_shape=(jax.ShapeDtypeStruct(x.shape, f32), jax.ShapeDtypeStruct(x.shape, bf16)),
        grid=(nb, pl.cdiv(seq_len, tm)),
        in_specs=[row_spec, row_spec, pl.BlockSpec((D_MODEL, D_MODEL), lambda b, i: (0, 0)), vec_spec, vec_spec],
        out_specs=(row_spec, row_spec),
        compiler_params=_params(("parallel", "parallel"), 32),
        name="proj_residual_layernorm",
    )(o, x, w, ln_g, ln_b)


def _router_kernel(x_ref, wh_ref, wl_ref, b_ref, gate_ref):
    xh, xl = _split_hi_lo(x_ref[...])
    wh = wh_ref[...]
    logits = (lax.dot_general(wh, xh, NT_DIMS, preferred_element_type=f32)
              + (lax.dot_general(wh, xl, NT_DIMS, preferred_element_type=f32)
                 + lax.dot_general(wl_ref[...], xh, NT_DIMS, preferred_element_type=f32)))
    s = jax.nn.sigmoid(logits)
    sel = s + b_ref[...]
    rows = [sel[e:e + 1, :] for e in range(N_EXPERTS)]
    scores = []
    for gidx in range(N_GROUPS):
        r = rows[gidx * EXPERTS_PER_GROUP:(gidx + 1) * EXPERTS_PER_GROUP]
        best = None
        for a in range(EXPERTS_PER_GROUP):
            for b in range(a + 1, EXPERTS_PER_GROUP):
                pair = r[a] + r[b]
                best = pair if best is None else jnp.maximum(best, pair)
        scores.append(best)
    picked = []
    for e in range(N_EXPERTS):
        gidx, loc = divmod(e, EXPERTS_PER_GROUP)
        win = None
        for j in range(N_GROUPS):
            if j == gidx:
                continue
            c = scores[gidx] > scores[j] if j < gidx else scores[gidx] >= scores[j]
            win = c if win is None else jnp.logical_and(win, c)
        rank = jnp.zeros_like(rows[e])
        for j in range(EXPERTS_PER_GROUP):
            if j == loc:
                continue
            o = rows[gidx * EXPERTS_PER_GROUP + j]
            ahead = o >= rows[e] if j < loc else o > rows[e]
            rank = rank + ahead.astype(f32)
        chosen = jnp.logical_and(win, rank < 1.5)
        picked.append(jnp.where(chosen, s[e:e + 1, :], 0.0))
    total = picked[0]
    for e in range(1, N_EXPERTS):
        total = total + picked[e]
    for e in range(N_EXPERTS):
        gate_ref[e:e + 1, :] = picked[e] / total


def _router(x, w_hi, w_lo, b_col, tm):
    n_rows = x.shape[0]
    return pl.pallas_call(
        _router_kernel,
        out_shape=jax.ShapeDtypeStruct((N_EXPERTS, n_rows), f32),
        grid=(pl.cdiv(n_rows, tm),),
        in_specs=[pl.BlockSpec((tm, D_MODEL), lambda i: (i, 0)),
                  pl.BlockSpec((N_EXPERTS, D_MODEL), lambda i: (0, 0)),
                  pl.BlockSpec((N_EXPERTS, D_MODEL), lambda i: (0, 0)),
                  pl.BlockSpec((N_EXPERTS, 1), lambda i: (0, 0))],
        out_specs=pl.BlockSpec((N_EXPERTS, tm), lambda i: (0, i)),
        compiler_params=_params(("parallel",), 32),
        name="moe_router",
    )(x, w_hi, w_lo, b_col)


def _row_copy(src_hbm, idx_ref, base, r, dst, sem):
    return pltpu.make_async_copy(src_hbm.at[pl.ds(idx_ref[base + r], 1)], dst.at[pl.ds(r, 1)], sem)


def _gather_rows(op, src_hbm, idx_ref, base, dst, sem, n_rows):
    def body(r, carry):
        getattr(_row_copy(src_hbm, idx_ref, base, r, dst, sem), op)()
        return carry

    lax.fori_loop(0, n_rows, body, 0, unroll=8)


def _moe_sorted_kernel(tg_ref, nu_ref, src_ref, x_hbm, gate_ref, wg_ref, wu_ref, wd_ref, y_ref,
                       xbuf, sem, xb_sc, acc_sc, *, tm):
    i = pl.program_id(0)
    j = pl.program_id(1)
    n_used = nu_ref[0]
    slot = i % 2
    live = i < n_used

    @pl.when(j == 0)
    def _():
        @pl.when(i == 0)
        def _():
            _gather_rows("start", x_hbm, src_ref, 0, xbuf.at[0], sem.at[0], tm)

        @pl.when(i + 1 < n_used)
        def _():
            _gather_rows("start", x_hbm, src_ref, (i + 1) * tm, xbuf.at[1 - slot], sem.at[1 - slot], tm)

        @pl.when(live)
        def _():
            _gather_rows("wait", x_hbm, src_ref, i * tm, xbuf.at[slot], sem.at[slot], tm)
            xb_sc[...] = xbuf[slot].astype(bf16)
            acc_sc[...] = jnp.zeros(acc_sc.shape, f32)

    @pl.when(live)
    def _():
        e = tg_ref[i] * EXPERTS_PER_GROUP + j
        x = xb_sc[...]
        a = jnp.dot(x, wg_ref[0].astype(bf16), preferred_element_type=f32)
        u = jnp.dot(x, wu_ref[0].astype(bf16), preferred_element_type=f32)
        hmid = (a * jax.nn.sigmoid(a) * u).astype(bf16)
        y = jnp.dot(hmid, wd_ref[0].astype(bf16), preferred_element_type=f32)
        gate = gate_ref[...]
        lane = lax.broadcasted_iota(jnp.int32, gate.shape, 1)
        gcol = jnp.sum(jnp.where(lane == e, gate, 0.0), axis=-1, keepdims=True)
        acc_sc[...] += gcol * y

    last = j == EXPERTS_PER_GROUP - 1

    @pl.when(jnp.logical_and(last, live))
    def _():
        y_ref[...] = acc_sc[...]

    @pl.when(jnp.logical_and(last, jnp.logical_not(live)))
    def _():
        y_ref[...] = jnp.zeros(y_ref.shape, f32)


def _moe_sorted(x, src, gate_sorted, tile_group, n_used, wg, wu, wd, tm):
    p_rows = src.shape[0]
    n_tiles = p_rows // tm

    def w_spec(shape):
        def index(i, j, tg, nu, sr):
            return (tg[i] * EXPERTS_PER_GROUP + jnp.where(i < nu[0], j, EXPERTS_PER_GROUP - 1), 0, 0)
        return pl.BlockSpec((1,) + shape, index)

    grid_spec = pltpu.PrefetchScalarGridSpec(
        num_scalar_prefetch=3,
        grid=(n_tiles, EXPERTS_PER_GROUP),
        in_specs=[pl.BlockSpec(memory_space=pl.ANY),
                  pl.BlockSpec((tm, N_EXPERTS), lambda i, j, tg, nu, sr: (i, 0)),
                  w_spec((D_MODEL, D_EXPERT)), w_spec((D_MODEL, D_EXPERT)), w_spec((D_EXPERT, D_MODEL))],
        out_specs=pl.BlockSpec((tm, D_MODEL), lambda i, j, tg, nu, sr: (i, 0)),
        scratch_shapes=[pltpu.VMEM((2, tm, D_MODEL), f32), pltpu.SemaphoreType.DMA((2,)),
                        pltpu.VMEM((tm, D_MODEL), bf16), pltpu.VMEM((tm, D_MODEL), f32)],
    )
    return pl.pallas_call(
        functools.partial(_moe_sorted_kernel, tm=tm),
        out_shape=jax.ShapeDtypeStruct((p_rows, D_MODEL), f32),
        grid_spec=grid_spec,
        compiler_params=_params(("arbitrary", "arbitrary"), 56),
        name="moe_sorted_experts",
    )(tile_group, n_used, src, x, gate_sorted, wg, wu, wd)


def _unpermute_ln_kernel(dest_ref, y_hbm, x_ref, g_ref, b_ref, xo_ref, xbo_ref, ybuf, sem, *, tm):
    i = pl.program_id(0)
    slot = i % 2

    @pl.when(i == 0)
    def _():
        _gather_rows("start", y_hbm, dest_ref, 0, ybuf.at[0], sem.at[0], tm)

    @pl.when(i + 1 < pl.num_programs(0))
    def _():
        _gather_rows("start", y_hbm, dest_ref, (i + 1) * tm, ybuf.at[1 - slot], sem.at[1 - slot], tm)

    _gather_rows("wait", y_hbm, dest_ref, i * tm, ybuf.at[slot], sem.at[slot], tm)
    out = _layer_norm(DN_ALPHA * x_ref[...] + ybuf[slot], g_ref[...], b_ref[...])
    xo_ref[...] = out
    xbo_ref[...] = out.astype(bf16)


def _unpermute_ln(y_sorted, dest, x, ln_g, ln_b, tm):
    n_rows = x.shape[0]
    row_spec = pl.BlockSpec((tm, D_MODEL), lambda i, d: (i, 0))
    vec_spec = pl.BlockSpec((1, D_MODEL), lambda i, d: (0, 0))
    grid_spec = pltpu.PrefetchScalarGridSpec(
        num_scalar_prefetch=1,
        grid=(pl.cdiv(n_rows, tm),),
        in_specs=[pl.BlockSpec(memory_space=pl.ANY), row_spec, vec_spec, vec_spec],
        out_specs=(row_spec, row_spec),
        scratch_shapes=[pltpu.VMEM((2, tm, D_MODEL), f32), pltpu.SemaphoreType.DMA((2,))],
    )
    return pl.pallas_call(
        functools.partial(_unpermute_ln_kernel, tm=tm),
        out_shape=(jax.ShapeDtypeStruct(x.shape, f32), jax.ShapeDtypeStruct(x.shape, bf16)),
        grid_spec=grid_spec,
        compiler_params=_params(("arbitrary",), 40),
        name="moe_unpermute_layernorm",
    )(dest, y_sorted, x, ln_g, ln_b)


def _moe_block(x, router_w, wg, wu, wd, ln_g, ln_b, tm_router, tm_moe, tm_ln):
    shp = x.shape
    xf = x.reshape(-1, D_MODEL)
    n_rows = xf.shape[0]
    gate = _router(xf, *router_w, tm_router).T
    grp = (jnp.argmax(gate, axis=1) // EXPERTS_PER_GROUP).astype(jnp.int32)
    onehot = (grp[:, None] == jnp.arange(N_GROUPS, dtype=jnp.int32)[None, :]).astype(jnp.int32)
    csum = jnp.cumsum(onehot, axis=0)
    rank = jnp.sum(csum * onehot, axis=1) - 1
    padded = ((csum[-1] + tm_moe - 1) // tm_moe) * tm_moe
    ends = jnp.cumsum(padded)
    dest = jnp.sum((ends - padded)[None, :] * onehot, axis=1) + rank
    n_tiles = pl.cdiv(n_rows, tm_moe) + N_GROUPS
    p_rows = n_tiles * tm_moe
    n_used = (ends[-1:] // tm_moe).astype(jnp.int32)
    tile_start = jnp.arange(n_tiles, dtype=jnp.int32) * tm_moe
    tile_group = jnp.minimum(jnp.sum((tile_start[:, None] >= ends[None, :]).astype(jnp.int32), axis=1), N_GROUPS - 1)
    src = jnp.zeros((p_rows,), jnp.int32).at[dest].set(jnp.arange(n_rows, dtype=jnp.int32), unique_indices=True)
    gate_sorted = jnp.zeros((p_rows, N_EXPERTS), f32).at[dest].set(gate, unique_indices=True)
    y_sorted = _moe_sorted(xf, src, gate_sorted, tile_group, n_used, wg, wu, wd, tm_moe)
    dest_pad = jnp.pad(dest, (0, pl.cdiv(n_rows, tm_ln) * tm_ln - n_rows))
    xo, xbo = _unpermute_ln(y_sorted, dest_pad, xf, ln_g, ln_b, tm_ln)
    return xo.reshape(shp), xbo.reshape(shp)


def _mlstm_proj_kernel(x_ref, w_ref, wgh_ref, wgl_ref, wgth_ref, wgtl_ref, z_ref, gcol_ref, grow_ref, *, seq_len, tm):
    i = pl.program_id(1)
    rows = i * tm + lax.broadcasted_iota(jnp.int32, (tm, 1), 0)
    x = x_ref[0]
    x = jnp.where(rows < seq_len, x, jnp.zeros_like(x))
    for part in range(3):
        sl = slice(part * D_MODEL, (part + 1) * D_MODEL)
        z_ref[0, :, sl] = jnp.dot(x, w_ref[:, sl], preferred_element_type=f32).astype(bf16)
    gcol_ref[0] = (jnp.dot(x, wgh_ref[...], preferred_element_type=f32)
                   + jnp.dot(x, wgl_ref[...], preferred_element_type=f32))
    grow_ref[0] = (lax.dot_general(wgth_ref[...], x, NT_DIMS, preferred_element_type=f32)
                   + lax.dot_general(wgtl_ref[...], x, NT_DIMS, preferred_element_type=f32))


def _mlstm_proj(x_b16, w_main, wg_cols, wg_rows, tm):
    nb, seq_len, _ = x_b16.shape
    nt = pl.cdiv(seq_len, tm)
    lp = nt * tm
    const = lambda shape: pl.BlockSpec(shape, lambda b, i: (0,) * len(shape))
    return pl.pallas_call(
        functools.partial(_mlstm_proj_kernel, seq_len=seq_len, tm=tm),
        out_shape=(jax.ShapeDtypeStruct((nb, lp, 3 * D_MODEL), bf16),
                   jax.ShapeDtypeStruct((nb, lp, LANES), f32),
                   jax.ShapeDtypeStruct((nb, 2 * HEADS, lp), f32)),
        grid=(nb, nt),
        in_specs=[pl.BlockSpec((1, tm, D_MODEL), lambda b, i: (b, i, 0)),
                  const((D_MODEL, 3 * D_MODEL)), const((D_MODEL, LANES)), const((D_MODEL, LANES)),
                  const((2 * HEADS, D_MODEL)), const((2 * HEADS, D_MODEL))],
        out_specs=(pl.BlockSpec((1, tm, 3 * D_MODEL), lambda b, i: (b, i, 0)),
                   pl.BlockSpec((1, tm, LANES), lambda b, i: (b, i, 0)),
                   pl.BlockSpec((1, 2 * HEADS, tm), lambda b, i: (b, 0, i))),
        compiler_params=_params(("parallel", "parallel"), 48),
        name="mlstm_in_proj",
    )(x_b16, w_main, *wg_cols, *wg_rows)


def _softcap(z):
    return GATE_SOFTCAP * jnp.tanh(z / GATE_SOFTCAP)


def _log_sigmoid(z):
    return jnp.minimum(z, 0.0) - jnp.log(1.0 + jnp.exp(-jnp.abs(z)))


def _mlstm_chunk_kernel(z_ref, gcol_ref, grow_ref, bcol_ref, brow_ref, g_ref, c0_ref, n0_ref, m0_ref,
                        h_ref, c_ref, n_ref, m_ref, c_sc, n_sc, m_sc, *, seq_len, tc):
    ci = pl.program_id(1)

    @pl.when(ci == 0)
    def _():
        c_sc[...] = c0_ref[0]
        n_sc[...] = n0_ref[0]
        m_sc[...] = m0_ref[0]

    t_col = ci * tc + lax.broadcasted_iota(jnp.int32, (tc, LANES), 0)
    lane = lax.broadcasted_iota(jnp.int32, (tc, LANES), 1)
    pre = _softcap(gcol_ref[0] + bcol_ref[...])
    gates_c = jnp.where(lane < HEADS, pre, _log_sigmoid(pre))
    neutral_c = jnp.where(lane < HEADS, NEG_BIG, 0.0)
    gates_c = jnp.where(t_col < seq_len, gates_c, neutral_c)
    t_row = ci * tc + lax.broadcasted_iota(jnp.int32, (2 * HEADS, tc), 1)
    sub = lax.broadcasted_iota(jnp.int32, (2 * HEADS, tc), 0)
    pre_r = _softcap(grow_ref[0] + brow_ref[...])
    gates_r = jnp.where(sub < HEADS, pre_r, _log_sigmoid(pre_r))
    neutral_r = jnp.where(sub < HEADS, NEG_BIG, 0.0)
    gates_r = jnp.where(t_row < seq_len, gates_r, neutral_r)

    ri = lax.broadcasted_iota(jnp.int32, (tc, tc), 0)
    cj = lax.broadcasted_iota(jnp.int32, (tc, tc), 1)
    causal = cj <= ri
    tril = jnp.where(causal, 1.0, 0.0).astype(bf16)
    triu = jnp.where(ri <= cj, 1.0, 0.0).astype(bf16)
    hi, lo = _split_hi_lo(gates_c)
    cum_c = jnp.dot(tril, hi, preferred_element_type=f32) + jnp.dot(tril, lo, preferred_element_type=f32)
    hi, lo = _split_hi_lo(gates_r)
    cum_r = jnp.dot(hi, triu, preferred_element_type=f32) + jnp.dot(lo, triu, preferred_element_type=f32)

    k_scale = M_DQK ** -0.5
    for h in range(HEADS):
        q = z_ref[0, :, h * M_DQK:(h + 1) * M_DQK]
        k = z_ref[0, :, HEADS * M_DQK + h * M_DQK:HEADS * M_DQK + (h + 1) * M_DQK]
        v = z_ref[0, :, D_MODEL + h * M_DV:D_MODEL + (h + 1) * M_DV]
        og = z_ref[0, :, 2 * D_MODEL + h * M_DV:2 * D_MODEL + (h + 1) * M_DV]
        b_col = cum_c[:, HEADS + h:HEADS + h + 1]
        a_col = gates_c[:, h:h + 1] - b_col
        a_row = gates_r[h:h + 1, :] - cum_r[HEADS + h:HEADS + h + 1, :]
        m0 = m_sc[h:h + 1, 0:1]
        c0 = c_sc[h]
        n0 = n_sc[h:h + 1, :]

        amat = jnp.where(causal, a_row, NEG_BIG)
        u = jnp.maximum(m0, jnp.max(amat, axis=-1, keepdims=True))
        dmat = jnp.exp(amat - u)
        s = lax.dot_general(q, k, NT_DIMS, preferred_element_type=f32) * k_scale
        w = s * dmat
        g_in = jnp.exp(m0 - u)
        num = (jnp.dot(w.astype(bf16), v, preferred_element_type=f32)
               + g_in * jnp.dot(q, c0.astype(bf16), preferred_element_type=f32))
        qn = jnp.sum(q.astype(f32) * n0, axis=-1, keepdims=True)
        den = jnp.sum(w, axis=-1, keepdims=True) + g_in * qn
        m_tok = b_col + u
        hval = num / jnp.maximum(jnp.abs(den), jnp.exp(-m_tok))
        hval = hval * lax.rsqrt(jnp.mean(hval * hval, axis=-1, keepdims=True) + RMS_EPS)
        hval = hval * g_ref[:, h * M_DV:(h + 1) * M_DV] * jax.nn.sigmoid(og.astype(f32))
        h_ref[0, :, h * M_DV:(h + 1) * M_DV] = hval.astype(bf16)

        u_last = u[tc - 1:tc, :]
        g0 = jnp.exp(m0 - u_last)
        decay = jnp.exp(a_col - u_last)
        kd = k.astype(f32) * (decay * k_scale)
        c_sc[h] = g0 * c0 + lax.dot_general(kd.astype(bf16), v, TN_DIMS, preferred_element_type=f32)
        n_sc[h:h + 1, :] = g0 * n0 + jnp.sum(kd, axis=0, keepdims=True)
        m_sc[h:h + 1, :] = jnp.broadcast_to(b_col[tc - 1:tc, :] + u_last, (1, LANES))

    @pl.when(ci == pl.num_programs(1) - 1)
    def _():
        c_ref[0] = c_sc[...]
        n_ref[0] = n_sc[...]
        m_ref[0] = m_sc[...]


def _mlstm_chunks(z, gcol, grow, bcol, brow, norm_g, c0t, n0, m0b, seq_len, tc):
    nb, lp, _ = z.shape
    nc = lp // tc
    per_b = lambda shape: pl.BlockSpec((1,) + shape, lambda b, c: (b,) + (0,) * len(shape))
    const = lambda shape: pl.BlockSpec(shape, lambda b, c: (0,) * len(shape))
    state_specs = (per_b((HEADS, M_DQK, M_DV)), per_b((HEADS, M_DQK)), per_b((HEADS, LANES)))
    return pl.pallas_call(
        functools.partial(_mlstm_chunk_kernel, seq_len=seq_len, tc=tc),
        out_shape=(jax.ShapeDtypeStruct((nb, lp, D_MODEL), bf16),
                   jax.ShapeDtypeStruct((nb, HEADS, M_DQK, M_DV), f32),
                   jax.ShapeDtypeStruct((nb, HEADS, M_DQK), f32),
                   jax.ShapeDtypeStruct((nb, HEADS, LANES), f32)),
        grid=(nb, nc),
        in_specs=[pl.BlockSpec((1, tc, 3 * D_MODEL), lambda b, c: (b, c, 0)),
                  pl.BlockSpec((1, tc, LANES), lambda b, c: (b, c, 0)),
                  pl.BlockSpec((1, 2 * HEADS, tc), lambda b, c: (b, 0, c)),
                  const((1, LANES)), const((2 * HEADS, 1)), const((1, D_MODEL))] + list(state_specs),
        out_specs=(pl.BlockSpec((1, tc, D_MODEL), lambda b, c: (b, c, 0)),) + state_specs,
        scratch_shapes=[pltpu.VMEM((HEADS, M_DQK, M_DV), f32), pltpu.VMEM((HEADS, M_DQK), f32),
                        pltpu.VMEM((HEADS, LANES), f32)],
        compiler_params=_params(("parallel", "arbitrary"), 32),
        name="mlstm_chunks",
    )(z, gcol, grow, bcol, brow, norm_g, c0t, n0, m0b)


ATT_TILE = 512
SAMPLE_TILE = 128
MLSTM_CHUNK = 128
PAGES_PER_STEP = 4
MOE_TILE = 1024
MOE_LN_TILE = 512
ROUTER_TILE = 512


def _attention_layer(xp, xp_b, xs, cache_k, cache_v, layer, page_table, w_in, lam_p, g, w_out, ln_g, ln_b, lam_init):
    nbd, n_new, _ = xs.shape
    seq_len = xp.shape[1]
    past = page_table.shape[1] * cache_k.shape[2]
    qb, kb, vb, kf, vf = _qkv_rope(xp_b, w_in.astype(bf16), jnp.arange(seq_len, dtype=jnp.int32), ATT_TILE)
    o = _flash_attention(qb, kb, vb, lam_p, g, lam_init, ATT_TILE)
    xp1, xp1_b = _proj_ln(o, xp, w_out.astype(bf16), ln_g, ln_b, ATT_TILE)
    pos_s = past + (jnp.arange(nbd * n_new, dtype=jnp.int32) % n_new)
    xs_rows = xs.reshape(1, nbd * n_new, D_MODEL)
    _, _, _, ksf, vsf, qs = _qkv_rope(xs_rows, w_in, pos_s, SAMPLE_TILE, q_f32=True)
    qs = qs.reshape(nbd, n_new, D_MODEL)
    ksf = ksf.reshape(nbd, n_new, D_MODEL)
    vsf = vsf.reshape(nbd, n_new, D_MODEL)
    n_pool, page = cache_k.shape[1], cache_k.shape[2]
    ck = cache_k.reshape(-1, page * HEADS, HEAD_W)
    cv = cache_v.reshape(-1, page * HEADS, HEAD_W)
    os_ = _decode_attention(qs, ksf, vsf, ck, cv, page_table, layer * n_pool, lam_p, g, lam_init, PAGES_PER_STEP)
    os_rows = os_.reshape(1, nbd * n_new, D_MODEL)
    xs1, xs1_b = _proj_ln(os_rows, xs_rows, w_out, ln_g, ln_b, SAMPLE_TILE)
    return (xp1, xp1_b, xs1.reshape(xs.shape), xs1_b.reshape(xs.shape),
            kf.reshape(kf.shape[:2] + (HEADS, HEAD_W)), vf.reshape(vf.shape[:2] + (HEADS, HEAD_W)),
            ksf.reshape(nbd, n_new, HEADS, HEAD_W), vsf.reshape(nbd, n_new, HEADS, HEAD_W))


def _mlstm_stream(x, x_b, w_main, wg_cols, wg_rows, bcol, brow, norm_g, c0t, n0, m0b, w_out, ln_g, ln_b):
    nb, seq_len, _ = x.shape
    short = seq_len < MLSTM_CHUNK
    if short:
        rows = nb * seq_len
        z, gcol, grow = _mlstm_proj(x_b.reshape(1, rows, D_MODEL), w_main, wg_cols, wg_rows, rows)
        pad = MLSTM_CHUNK - seq_len
        z = jnp.pad(z[0, :rows].reshape(nb, seq_len, 3 * D_MODEL), [(0, 0), (0, pad), (0, 0)])
        gcol = jnp.pad(gcol[0, :rows].reshape(nb, seq_len, LANES), [(0, 0), (0, pad), (0, 0)])
        grow = jnp.transpose(grow[0, :, :rows].reshape(2 * HEADS, nb, seq_len), (1, 0, 2))
        grow = jnp.pad(grow, [(0, 0), (0, 0), (0, pad)])
    else:
        z, gcol, grow = _mlstm_proj(x_b, w_main, wg_cols, wg_rows, MLSTM_CHUNK)
    hb, ct, n, mb = _mlstm_chunks(z, gcol, grow, bcol, brow, norm_g, c0t, n0, m0b, seq_len, MLSTM_CHUNK)
    if short:
        hb = hb[:, :seq_len].reshape(1, rows, D_MODEL)
        x1, x1_b = _proj_ln(hb, x.reshape(1, rows, D_MODEL), w_out, ln_g, ln_b, rows)
        x1, x1_b = x1.reshape(x.shape), x1_b.reshape(x.shape)
    else:
        x1, x1_b = _proj_ln(hb, x, w_out, ln_g, ln_b, ATT_TILE)
    return x1, x1_b, jnp.swapaxes(ct, -1, -2), n, mb[..., 0]


def kernel(x_prompt, x_sample, cache_k, cache_v, state_C, state_n, state_m, page_table, meta_tokens, w_attn_in, lambda_q1, lambda_k1, lambda_q2, lambda_k2, subln_g, w_attn_out, w_mlstm_in, b_mlstm_if, mlstm_norm_g, w_mlstm_out, w_router, b_router, w_exp_gate, w_exp_up, w_exp_down, ln_g, ln_b):
    nb = x_prompt.shape[0]
    nbd = x_sample.shape[0]
    meta = jnp.broadcast_to(meta_tokens.astype(x_prompt.dtype)[None], (nb, N_META, D_MODEL))
    xp = jnp.concatenate([meta, x_prompt], axis=1)
    xs = x_sample
    xp_b = xp.astype(bf16)
    router_w = _split_hi_lo(w_router.T) + (b_router.astype(f32).reshape(N_EXPERTS, 1),)
    prompt_moe_tile = min(MOE_TILE, xp.shape[0] * xp.shape[1])
    sample_rows = nbd * xs.shape[1]

    def moe_pair(i, xp, xs):
        args = (router_w, w_exp_gate[i], w_exp_up[i], w_exp_down[i], ln_g[i, 1:2], ln_b[i, 1:2])
        xp, xp_b = _moe_block(xp, *args, ROUTER_TILE, prompt_moe_tile, min(MOE_LN_TILE, prompt_moe_tile))
        xs, xs_b = _moe_block(xs, *args, sample_rows, sample_rows, sample_rows)
        return xp, xp_b, xs, xs_b

    lam_init = 0.8 - 0.6 * math.exp(-0.3 * 0)
    lam_p = jnp.stack([lambda_q1[0], lambda_k1[0], lambda_q2[0], lambda_k2[0]]).astype(f32)
    (xp, xp_b, xs, xs_b, k_p, v_p, k_s, v_s) = _attention_layer(
        xp, xp_b, xs, cache_k, cache_v, 0, page_table, w_attn_in[0], lam_p,
        subln_g[0].reshape(1, HEAD_W), w_attn_out[0], ln_g[0, 0:1], ln_b[0, 0:1], lam_init)
    xp, xp_b, xs, xs_b = moe_pair(0, xp, xs)

    w_in = w_mlstm_in[0]
    w_main = w_in[:, :3 * D_MODEL].astype(bf16)
    w_gate = w_in[:, 3 * D_MODEL:]
    wg_cols = _split_hi_lo(jnp.pad(w_gate, [(0, 0), (0, LANES - 2 * HEADS)]))
    wg_rows = _split_hi_lo(w_gate.T)
    b_if = b_mlstm_if[0].astype(f32).reshape(2 * HEADS)
    bcol = jnp.pad(b_if, (0, LANES - 2 * HEADS)).reshape(1, LANES)
    brow = b_if.reshape(2 * HEADS, 1)
    norm_g = mlstm_norm_g[0].reshape(1, D_MODEL)
    common = (w_main, wg_cols, wg_rows, bcol, brow, norm_g)
    tail = (w_mlstm_out[0].astype(bf16), ln_g[1, 0:1], ln_b[1, 0:1])
    zero_state = (jnp.zeros((nb, HEADS, M_DQK, M_DV), f32), jnp.zeros((nb, HEADS, M_DQK), f32),
                  jnp.zeros((nb, HEADS, LANES), f32))
    xp, xp_b, c_p, n_p, m_p = _mlstm_stream(xp, xp_b, *common, *zero_state, *tail)
    sample_state = (jnp.swapaxes(state_C[0].astype(f32), -1, -2), state_n[0].astype(f32),
                    jnp.broadcast_to(state_m[0].astype(f32)[..., None], (nbd, HEADS, LANES)))
    xs, xs_b, c_s, n_s, m_s = _mlstm_stream(xs, xs_b, *common, *sample_state, *tail)
    xp, xp_b, xs, xs_b = moe_pair(1, xp, xs)

    return (xp[:, N_META:], xs, k_p[None], v_p[None], k_s[None], v_s[None],
            c_p[None], n_p[None], m_p[None], c_s[None], n_s[None], m_s[None])
```

```python
import functools
import math

import jax
import jax.numpy as jnp
from jax import lax
from jax.experimental import pallas as pl
from jax.experimental.pallas import tpu as pltpu

f32 = jnp.float32
bf16 = jnp.bfloat16

D_MODEL = 1024
N_META = 16
HEADS = 8
HEAD_W = 128
SUB_W = 64
ATT_SCALE = SUB_W ** -0.5
Q_SCALE = ATT_SCALE * math.log2(math.e)
ROT_DIM = 16
ROPE_THETA = 500000.0
M_DQK = 64
M_DV = 128
GATE_SOFTCAP = 15.0
N_EXPERTS = 16
N_GROUPS = 4
EXPERTS_PER_GROUP = 4
D_EXPERT = 512
DEPTH = 2
DN_ALPHA = (2 * DEPTH) ** 0.25
LN_EPS = 1e-5
RMS_EPS = 1e-6
NEG_BIG = -1e30

LANES = 128
MIB = 1024 * 1024

NT_DIMS = (((1,), (1,)), ((), ()))
TN_DIMS = (((0,), (0,)), ((), ()))


def _params(semantics, vmem_mib):
    return pltpu.CompilerParams(dimension_semantics=semantics, vmem_limit_bytes=vmem_mib * MIB)


def _split_hi_lo(a):
    hi = a.astype(bf16)
    lo = (a - hi.astype(f32)).astype(bf16)
    return hi, lo


def _matmul(x, w):
    if x.dtype == bf16:
        return jnp.dot(x, w, preferred_element_type=f32)
    xh, xl = _split_hi_lo(x)
    wh, wl = _split_hi_lo(w)
    return (jnp.dot(xh, wh, preferred_element_type=f32)
            + (jnp.dot(xl, wh, preferred_element_type=f32) + jnp.dot(xh, wl, preferred_element_type=f32)))


def _layer_norm(r, g, b):
    mu = jnp.mean(r, axis=-1, keepdims=True)
    rc = r - mu
    var = jnp.mean(rc * rc, axis=-1, keepdims=True)
    return rc * lax.rsqrt(var + LN_EPS) * g + b


def _qkv_rope_kernel(x_ref, w_ref, c_ref, s1_ref, s2_ref,
                     qb_ref, kb_ref, vb_ref, kf_ref, vf_ref, *maybe_qf_ref, seq_len, tm):
    i = pl.program_id(1)
    rows = i * tm + lax.broadcasted_iota(jnp.int32, (tm, 1), 0)
    valid = rows < seq_len
    x = x_ref[0]
    c = c_ref[...]
    s1 = s1_ref[...]
    s2 = s2_ref[...]
    for part in range(3):
        y = _matmul(x, w_ref[:, part * D_MODEL:(part + 1) * D_MODEL])
        for h in range(HEADS):
            sl = slice(h * HEAD_W, (h + 1) * HEAD_W)
            yh = y[:, sl]
            if part < 2:
                yh = yh * c + pltpu.roll(yh, LANES - ROT_DIM // 2, 1) * s1 + pltpu.roll(yh, ROT_DIM // 2, 1) * s2
            if part == 0:
                qb_ref[0, :, sl] = jnp.where(valid, yh * Q_SCALE, 0.0).astype(bf16)
                for qf_ref in maybe_qf_ref:
                    qf_ref[0, :, sl] = yh * Q_SCALE
            elif part == 1:
                kf_ref[0, :, sl] = yh
                kb_ref[0, :, sl] = jnp.where(valid, yh, 0.0).astype(bf16)
            else:
                vf_ref[0, :, sl] = yh
                vb_ref[0, :, sl] = jnp.where(valid, yh, 0.0).astype(bf16)


def _rope_tables(pos):
    half = ROT_DIM // 2
    inv_freq = ROPE_THETA ** (-jnp.arange(half, dtype=f32) * (2.0 / ROT_DIM))
    ang = pos.astype(f32)[:, None] * inv_freq[None, :]
    cos = jnp.cos(ang)
    sin = jnp.sin(ang)
    t = pos.shape[0]
    ones = jnp.ones((t, SUB_W - ROT_DIM), f32)
    zeros8 = jnp.zeros((t, half), f32)
    zeros48 = jnp.zeros((t, SUB_W - ROT_DIM), f32)
    c = jnp.concatenate([cos, cos, ones], axis=1)
    s1 = jnp.concatenate([-sin, zeros8, zeros48], axis=1)
    s2 = jnp.concatenate([zeros8, sin, zeros48], axis=1)
    return tuple(jnp.concatenate([a, a], axis=1) for a in (c, s1, s2))


def _qkv_rope(x, w, pos, tm, q_f32=False):
    nb, seq_len, _ = x.shape
    nt = pl.cdiv(seq_len, tm)
    lp = nt * tm
    c, s1, s2 = _rope_tables(pos)
    row_spec = pl.BlockSpec((1, tm, D_MODEL), lambda b, i: (b, i, 0))
    tab_spec = pl.BlockSpec((tm, HEAD_W), lambda b, i: (i, 0))
    pad_shape = jax.ShapeDtypeStruct((nb, lp, D_MODEL), bf16)
    out_shape = jax.ShapeDtypeStruct((nb, seq_len, D_MODEL), f32)
    n_out = 6 if q_f32 else 5
    return pl.pallas_call(
        functools.partial(_qkv_rope_kernel, seq_len=seq_len, tm=tm),
        out_shape=(pad_shape, pad_shape, pad_shape) + (out_shape,) * (n_out - 3),
        grid=(nb, nt),
        in_specs=[row_spec, pl.BlockSpec((D_MODEL, 3 * D_MODEL), lambda b, i: (0, 0)), tab_spec, tab_spec, tab_spec],
        out_specs=(row_spec,) * n_out,
        compiler_params=_params(("parallel", "parallel"), 48),
        name="qkv_rope",
    )(x, w, c, s1, s2)


def _diff_lambda(lam_ref, lam_init):
    p = lam_ref[...]
    l1 = jnp.sum(p[0:1] * p[1:2], axis=-1, keepdims=True)
    l2 = jnp.sum(p[2:3] * p[3:4], axis=-1, keepdims=True)
    return jnp.exp(l1) - jnp.exp(l2) + lam_init


def _diff_finish(o1, o2, lam, g, lam_init):
    o = o1 - lam * o2
    o = o * lax.rsqrt(jnp.mean(o * o, axis=-1, keepdims=True) + RMS_EPS)
    return o * g * (1.0 - lam_init)


def _flash_kernel(lam_ref, g_ref, q_ref, k_ref, v_ref, o_ref, qq_sc, m_sc, l_sc, acc_sc, *, lam_init, tq):
    qi = pl.program_id(2)
    q = q_ref[0]
    lane = lax.broadcasted_iota(jnp.int32, (tq, HEAD_W), 1)
    zero = jnp.zeros_like(q)
    qq_sc[0:tq, :] = jnp.where(lane < SUB_W, q, zero)
    qq_sc[tq:2 * tq, :] = jnp.where(lane >= SUB_W, q, zero)
    m_sc[...] = jnp.full(m_sc.shape, NEG_BIG, f32)
    l_sc[...] = jnp.zeros(l_sc.shape, f32)
    acc_sc[...] = jnp.zeros(acc_sc.shape, f32)

    def step(kstart, masked):
        k = k_ref[0, pl.ds(kstart, tq), :]
        v = v_ref[0, pl.ds(kstart, tq), :]
        s = lax.dot_general(qq_sc[...], k, NT_DIMS, preferred_element_type=f32)
        if masked:
            r = lax.broadcasted_iota(jnp.int32, (2 * tq, tq), 0)
            r = jnp.where(r >= tq, r - tq, r)
            cidx = lax.broadcasted_iota(jnp.int32, (2 * tq, tq), 1)
            s = jnp.where(cidx <= r, s, NEG_BIG)
        m_prev = m_sc[...]
        m_new = jnp.maximum(m_prev, jnp.max(s, axis=-1, keepdims=True))
        alpha = jnp.exp2(m_prev - m_new)
        parts = [jnp.exp2(s[:, j * LANES:(j + 1) * LANES] - m_new) for j in range(tq // LANES)]
        psum = parts[0]
        for part in parts[1:]:
            psum = psum + part
        p = jnp.concatenate([part.astype(bf16) for part in parts], axis=1)
        l_sc[...] = alpha * l_sc[...] + psum
        acc_sc[...] = alpha * acc_sc[...] + jnp.dot(p, v, preferred_element_type=f32)
        m_sc[...] = m_new

    def body(ki, carry):
        step(pl.multiple_of(ki * tq, tq), False)
        return carry

    lax.fori_loop(0, qi, body, 0)
    step(pl.multiple_of(qi * tq, tq), True)

    lam = _diff_lambda(lam_ref, lam_init)
    l = jnp.sum(l_sc[...], axis=-1, keepdims=True)
    o1 = acc_sc[0:tq, :] / l[0:tq, :]
    o2 = acc_sc[tq:2 * tq, :] / l[tq:2 * tq, :]
    o_ref[0] = _diff_finish(o1, o2, lam, g_ref[...], lam_init).astype(bf16)


def _flash_attention(qb, kb, vb, lam_p, g, lam_init, tq):
    nb, lp, _ = qb.shape
    nq = lp // tq
    q_spec = pl.BlockSpec((1, tq, HEAD_W), lambda b, h, i: (b, i, h))
    kv_spec = pl.BlockSpec((1, lp, HEAD_W), lambda b, h, i: (b, 0, h))
    return pl.pallas_call(
        functools.partial(_flash_kernel, lam_init=lam_init, tq=tq),
        out_shape=jax.ShapeDtypeStruct((nb, lp, D_MODEL), bf16),
        grid=(nb, HEADS, nq),
        in_specs=[pl.BlockSpec((4, SUB_W), lambda b, h, i: (0, 0)),
                  pl.BlockSpec((1, HEAD_W), lambda b, h, i: (0, 0)),
                  q_spec, kv_spec, kv_spec],
        out_specs=q_spec,
        scratch_shapes=[pltpu.VMEM((2 * tq, HEAD_W), bf16), pltpu.VMEM((2 * tq, LANES), f32),
                        pltpu.VMEM((2 * tq, LANES), f32), pltpu.VMEM((2 * tq, HEAD_W), f32)],
        compiler_params=_params(("parallel", "parallel", "arbitrary"), 48),
        name="flash_diff_attention",
    )(lam_p, g, qb, kb, vb)


def _decode_kernel(pt_ref, lam_ref, g_ref, q_ref, kn_ref, vn_ref, *rest, lam_init, pages_per_step, n_new):
    k_refs = rest[:pages_per_step]
    v_refs = rest[pages_per_step:2 * pages_per_step]
    o_ref = rest[2 * pages_per_step]
    m_sc, l_sc, acc_sc = rest[2 * pages_per_step + 1:]
    del pt_ref
    p_idx = pl.program_id(1)
    rph = 2 * n_new
    n_rows = HEADS * rph
    page = k_refs[0].shape[1] // HEADS

    @pl.when(p_idx == 0)
    def _():
        m_sc[...] = jnp.full(m_sc.shape, NEG_BIG, f32)
        l_sc[...] = jnp.zeros(l_sc.shape, f32)
        acc_sc[...] = jnp.zeros(acc_sc.shape, f32)

    def head_rows(ref, h):
        return _split_hi_lo(ref[0, pl.ds(h, page, stride=HEADS), :])

    def fold(a):
        return a[0:rph] + a[rph:2 * rph]

    def step(kv_refs, masked):
        scores = []
        for k_ref, _ in kv_refs:
            per_head = []
            for h in range(HEADS):
                q2 = q_ref[0, h]
                k_hi, k_lo = head_rows(k_ref, h)
                a = (lax.dot_general(q2, k_hi, NT_DIMS, preferred_element_type=f32)
                     + lax.dot_general(q2, k_lo, NT_DIMS, preferred_element_type=f32))
                per_head.append(fold(a))
            s = jnp.concatenate(per_head, axis=0)
            if masked:
                r = lax.broadcasted_iota(jnp.int32, (n_rows, page), 0)
                tok = r % n_new
                cidx = lax.broadcasted_iota(jnp.int32, (n_rows, page), 1)
                s = jnp.where(cidx <= tok, s, NEG_BIG)
            scores.append(s)
        m_prev = m_sc[...]
        m_new = m_prev
        for s in scores:
            m_new = jnp.maximum(m_new, jnp.max(s, axis=-1, keepdims=True))
        alpha = jnp.exp2(m_prev - m_new)
        l_new = alpha * l_sc[...]
        pv = [None] * HEADS
        for s, (_, v_ref) in zip(scores, kv_refs):
            p = jnp.exp2(s - m_new)
            l_new = l_new + jnp.sum(p, axis=-1, keepdims=True)
            for h in range(HEADS):
                p2 = jnp.concatenate(_split_hi_lo(p[h * rph:(h + 1) * rph]), axis=0)
                v_hi, v_lo = head_rows(v_ref, h)
                part = fold(jnp.dot(p2, v_hi, preferred_element_type=f32)
                            + jnp.dot(p2, v_lo, preferred_element_type=f32))
                pv[h] = part if pv[h] is None else pv[h] + part
        l_sc[...] = l_new
        for h in range(HEADS):
            rs = slice(h * rph, (h + 1) * rph)
            acc_sc[rs, :] = alpha[rs, :] * acc_sc[rs, :] + pv[h]
        m_sc[...] = m_new

    step(list(zip(k_refs, v_refs)), False)

    @pl.when(p_idx == pl.num_programs(1) - 1)
    def _():
        step([(kn_ref, vn_ref)], True)
        lam = _diff_lambda(lam_ref, lam_init)
        o = acc_sc[...] / l_sc[...]
        for h in range(HEADS):
            o1 = o[h * rph:h * rph + n_new, :]
            o2 = o[h * rph + n_new:(h + 1) * rph, :]
            o_ref[0, :, h * HEAD_W:(h + 1) * HEAD_W] = _diff_finish(o1, o2, lam, g_ref[...], lam_init)


def _decode_attention(q, k_new, v_new, cache_k, cache_v, page_table, page_base, lam_p, g, lam_init,
                      pages_per_step):
    nbd, n_new, _ = q.shape
    rph = 2 * n_new
    assert rph % 8 == 0, "each head's (sub-head, token) rows must fill whole sublane tiles"
    page_rows = cache_k.shape[1]
    n_pages = page_table.shape[1]
    n_rows = HEADS * rph
    qh = q.reshape(nbd, n_new, HEADS, HEAD_W).transpose(0, 2, 1, 3)
    lane_sub = jnp.arange(HEAD_W, dtype=jnp.int32) // SUB_W
    q_sub = [jnp.where(lane_sub == c, qh, jnp.zeros_like(qh)) for c in range(2)]
    q2 = jnp.concatenate(_split_hi_lo(jnp.concatenate(q_sub, axis=2)), axis=2)
    new_rows = lambda a: jnp.pad(a.reshape(nbd, n_new * HEADS, HEAD_W), [(0, 0), (0, page_rows - n_new * HEADS), (0, 0)])
    kn = new_rows(k_new)
    vn = new_rows(v_new)
    steps = n_pages // pages_per_step

    def page_spec(j):
        return pl.BlockSpec((1, page_rows, HEAD_W),
                            lambda b, p, pt: (page_base + pt[b, p * pages_per_step + j], 0, 0))

    per_seq = lambda shape: pl.BlockSpec((1,) + shape, lambda b, p, pt: (b,) + (0,) * len(shape))
    grid_spec = pltpu.PrefetchScalarGridSpec(
        num_scalar_prefetch=1,
        grid=(nbd, steps),
        in_specs=[pl.BlockSpec((4, SUB_W), lambda b, p, pt: (0, 0)),
                  pl.BlockSpec((1, HEAD_W), lambda b, p, pt: (0, 0)),
                  per_seq((HEADS, 2 * rph, HEAD_W)), per_seq((page_rows, HEAD_W)), per_seq((page_rows, HEAD_W))]
                 + [page_spec(j) for j in range(pages_per_step)]
                 + [page_spec(j) for j in range(pages_per_step)],
        out_specs=per_seq((n_new, D_MODEL)),
        scratch_shapes=[pltpu.VMEM((n_rows, 1), f32), pltpu.VMEM((n_rows, 1), f32),
                        pltpu.VMEM((n_rows, HEAD_W), f32)],
    )
    return pl.pallas_call(
        functools.partial(_decode_kernel, lam_init=lam_init, pages_per_step=pages_per_step, n_new=n_new),
        out_shape=jax.ShapeDtypeStruct((nbd, n_new, D_MODEL), f32),
        grid_spec=grid_spec,
        compiler_params=_params(("parallel", "arbitrary"), 40),
        name="paged_decode_attention",
    )(page_table, lam_p, g, q2, kn, vn, *([cache_k] * pages_per_step), *([cache_v] * pages_per_step))


def _proj_ln_kernel(o_ref, x_ref, w_ref, g_ref, b_ref, xo_ref, xb_ref):
    y = _matmul(o_ref[0], w_ref[...])
    out = _layer_norm(DN_ALPHA * x_ref[0] + y, g_ref[...], b_ref[...])
    xo_ref[0] = out
    xb_ref[0] = out.astype(bf16)


def _proj_ln(o, x, w, ln_g, ln_b, tm):
    nb, seq_len, _ = x.shape
    row_spec = pl.BlockSpec((1, tm, D_MODEL), lambda b, i: (b, i, 0))
    vec_spec = pl.BlockSpec((1, D_MODEL), lambda b, i: (0, 0))
    return pl.pallas_call(
        _proj_ln_kernel,
        out_shape=(jax.ShapeDtypeStruct(x.shape, f32), jax.ShapeDtypeStruct(x.shape, bf16)),
        grid=(nb, pl.cdiv(seq_len, tm)),
        in_specs=[row_spec, row_spec, pl.BlockSpec((D_MODEL, D_MODEL), lambda b, i: (0, 0)), vec_spec, vec_spec],
        out_specs=(row_spec, row_spec),
        compiler_params=_params(("parallel", "parallel"), 32),
        name="proj_residual_layernorm",
    )(o, x, w, ln_g, ln_b)


def _router_kernel(x_ref, wh_ref, wl_ref, b_ref, gate_ref):
    xh, xl = _split_hi_lo(x_ref[...])
    wh = wh_ref[...]
    logits = (lax.dot_general(wh, xh, NT_DIMS, preferred_element_type=f32)
              + (lax.dot_general(wh, xl, NT_DIMS, preferred_element_type=f32)
                 + lax.dot_general(wl_ref[...], xh, NT_DIMS, preferred_element_type=f32)))
    s = jax.nn.sigmoid(logits)
    sel = s + b_ref[...]
    rows = [sel[e:e + 1, :] for e in range(N_EXPERTS)]
    scores = []
    for gidx in range(N_GROUPS):
        r = rows[gidx * EXPERTS_PER_GROUP:(gidx + 1) * EXPERTS_PER_GROUP]
        best = None
        for a in range(EXPERTS_PER_GROUP):
            for b in range(a + 1, EXPERTS_PER_GROUP):
                pair = r[a] + r[b]
                best = pair if best is None else jnp.maximum(best, pair)
        scores.append(best)
    picked = []
    for e in range(N_EXPERTS):
        gidx, loc = divmod(e, EXPERTS_PER_GROUP)
        win = None
        for j in range(N_GROUPS):
            if j == gidx:
                continue
            c = scores[gidx] > scores[j] if j < gidx else scores[gidx] >= scores[j]
            win = c if win is None else jnp.logical_and(win, c)
        rank = jnp.zeros_like(rows[e])
        for j in range(EXPERTS_PER_GROUP):
            if j == loc:
                continue
            o = rows[gidx * EXPERTS_PER_GROUP + j]
            ahead = o >= rows[e] if j < loc else o > rows[e]
            rank = rank + ahead.astype(f32)
        chosen = jnp.logical_and(win, rank < 1.5)
        picked.append(jnp.where(chosen, s[e:e + 1, :], 0.0))
    total = picked[0]
    for e in range(1, N_EXPERTS):
        total = total + picked[e]
    for e in range(N_EXPERTS):
        gate_ref[e:e + 1, :] = picked[e] / total


def _router(x, w_hi, w_lo, b_col, tm):
    n_rows = x.shape[0]
    return pl.pallas_call(
        _router_kernel,
        out_shape=jax.ShapeDtypeStruct((N_EXPERTS, n_rows), f32),
        grid=(pl.cdiv(n_rows, tm),),
        in_specs=[pl.BlockSpec((tm, D_MODEL), lambda i: (i, 0)),
                  pl.BlockSpec((N_EXPERTS, D_MODEL), lambda i: (0, 0)),
                  pl.BlockSpec((N_EXPERTS, D_MODEL), lambda i: (0, 0)),
                  pl.BlockSpec((N_EXPERTS, 1), lambda i: (0, 0))],
        out_specs=pl.BlockSpec((N_EXPERTS, tm), lambda i: (0, i)),
        compiler_params=_params(("parallel",), 32),
        name="moe_router",
    )(x, w_hi, w_lo, b_col)


def _row_copy(src_hbm, idx_ref, base, r, dst, sem):
    return pltpu.make_async_copy(src_hbm.at[pl.ds(idx_ref[base + r], 1)], dst.at[pl.ds(r, 1)], sem)


def _gather_rows(op, src_hbm, idx_ref, base, dst, sem, n_rows):
    def body(r, carry):
        getattr(_row_copy(src_hbm, idx_ref, base, r, dst, sem), op)()
        return carry

    lax.fori_loop(0, n_rows, body, 0, unroll=8)


def _moe_sorted_kernel(tg_ref, nu_ref, src_ref, x_hbm, gate_ref, wg_ref, wu_ref, wd_ref, y_ref,
                       xbuf, sem, xb_sc, acc_sc, *, tm):
    i = pl.program_id(0)
    j = pl.program_id(1)
    n_used = nu_ref[0]
    slot = i % 2
    live = i < n_used

    @pl.when(j == 0)
    def _():
        @pl.when(i == 0)
        def _():
            _gather_rows("start", x_hbm, src_ref, 0, xbuf.at[0], sem.at[0], tm)

        @pl.when(i + 1 < n_used)
        def _():
            _gather_rows("start", x_hbm, src_ref, (i + 1) * tm, xbuf.at[1 - slot], sem.at[1 - slot], tm)

        @pl.when(live)
        def _():
            _gather_rows("wait", x_hbm, src_ref, i * tm, xbuf.at[slot], sem.at[slot], tm)
            xb_sc[...] = xbuf[slot].astype(bf16)
            acc_sc[...] = jnp.zeros(acc_sc.shape, f32)

    @pl.when(live)
    def _():
        e = tg_ref[i] * EXPERTS_PER_GROUP + j
        x = xb_sc[...]
        a = jnp.dot(x, wg_ref[0].astype(bf16), preferred_element_type=f32)
        u = jnp.dot(x, wu_ref[0].astype(bf16), preferred_element_type=f32)
        hmid = (a * jax.nn.sigmoid(a) * u).astype(bf16)
        y = jnp.dot(hmid, wd_ref[0].astype(bf16), preferred_element_type=f32)
        gate = gate_ref[...]
        lane = lax.broadcasted_iota(jnp.int32, gate.shape, 1)
        gcol = jnp.sum(jnp.where(lane == e, gate, 0.0), axis=-1, keepdims=True)
        acc_sc[...] += gcol * y

    last = j == EXPERTS_PER_GROUP - 1

    @pl.when(jnp.logical_and(last, live))
    def _():
        y_ref[...] = acc_sc[...]

    @pl.when(jnp.logical_and(last, jnp.logical_not(live)))
    def _():
        y_ref[...] = jnp.zeros(y_ref.shape, f32)


def _moe_sorted(x, src, gate_sorted, tile_group, n_used, wg, wu, wd, tm):
    p_rows = src.shape[0]
    n_tiles = p_rows // tm

    def w_spec(shape):
        def index(i, j, tg, nu, sr):
            return (tg[i] * EXPERTS_PER_GROUP + jnp.where(i < nu[0], j, EXPERTS_PER_GROUP - 1), 0, 0)
        return pl.BlockSpec((1,) + shape, index)

    grid_spec = pltpu.PrefetchScalarGridSpec(
        num_scalar_prefetch=3,
        grid=(n_tiles, EXPERTS_PER_GROUP),
        in_specs=[pl.BlockSpec(memory_space=pl.ANY),
                  pl.BlockSpec((tm, N_EXPERTS), lambda i, j, tg, nu, sr: (i, 0)),
                  w_spec((D_MODEL, D_EXPERT)), w_spec((D_MODEL, D_EXPERT)), w_spec((D_EXPERT, D_MODEL))],
        out_specs=pl.BlockSpec((tm, D_MODEL), lambda i, j, tg, nu, sr: (i, 0)),
        scratch_shapes=[pltpu.VMEM((2, tm, D_MODEL), f32), pltpu.SemaphoreType.DMA((2,)),
                        pltpu.VMEM((tm, D_MODEL), bf16), pltpu.VMEM((tm, D_MODEL), f32)],
    )
    return pl.pallas_call(
        functools.partial(_moe_sorted_kernel, tm=tm),
        out_shape=jax.ShapeDtypeStruct((p_rows, D_MODEL), f32),
        grid_spec=grid_spec,
        compiler_params=_params(("arbitrary", "arbitrary"), 56),
        name="moe_sorted_experts",
    )(tile_group, n_used, src, x, gate_sorted, wg, wu, wd)


def _unpermute_ln_kernel(dest_ref, y_hbm, x_ref, g_ref, b_ref, xo_ref, xbo_ref, ybuf, sem, *, tm):
    i = pl.program_id(0)
    slot = i % 2

    @pl.when(i == 0)
    def _():
        _gather_rows("start", y_hbm, dest_ref, 0, ybuf.at[0], sem.at[0], tm)

    @pl.when(i + 1 < pl.num_programs(0))
    def _():
        _gather_rows("start", y_hbm, dest_ref, (i + 1) * tm, ybuf.at[1 - slot], sem.at[1 - slot], tm)

    _gather_rows("wait", y_hbm, dest_ref, i * tm, ybuf.at[slot], sem.at[slot], tm)
    out = _layer_norm(DN_ALPHA * x_ref[...] + ybuf[slot], g_ref[...], b_ref[...])
    xo_ref[...] = out
    xbo_ref[...] = out.astype(bf16)


def _unpermute_ln(y_sorted, dest, x, ln_g, ln_b, tm):
    n_rows = x.shape[0]
    row_spec = pl.BlockSpec((tm, D_MODEL), lambda i, d: (i, 0))
    vec_spec = pl.BlockSpec((1, D_MODEL), lambda i, d: (0, 0))
    grid_spec = pltpu.PrefetchScalarGridSpec(
        num_scalar_prefetch=1,
        grid=(pl.cdiv(n_rows, tm),),
        in_specs=[pl.BlockSpec(memory_space=pl.ANY), row_spec, vec_spec, vec_spec],
        out_specs=(row_spec, row_spec),
        scratch_shapes=[pltpu.VMEM((2, tm, D_MODEL), f32), pltpu.SemaphoreType.DMA((2,))],
    )
    return pl.pallas_call(
        functools.partial(_unpermute_ln_kernel, tm=tm),
        out_shape=(jax.ShapeDtypeStruct(x.shape, f32), jax.ShapeDtypeStruct(x.shape, bf16)),
        grid_spec=grid_spec,
        compiler_params=_params(("arbitrary",), 40),
        name="moe_unpermute_layernorm",
    )(dest, y_sorted, x, ln_g, ln_b)


def _moe_block(x, router_w, wg, wu, wd, ln_g, ln_b, tm_router, tm_moe, tm_ln):
    shp = x.shape
    xf = x.reshape(-1, D_MODEL)
    n_rows = xf.shape[0]
    gate = _router(xf, *router_w, tm_router).T
    grp = (jnp.argmax(gate, axis=1) // EXPERTS_PER_GROUP).astype(jnp.int32)
    onehot = (grp[:, None] == jnp.arange(N_GROUPS, dtype=jnp.int32)[None, :]).astype(jnp.int32)
    csum = jnp.cumsum(onehot, axis=0)
    rank = jnp.sum(csum * onehot, axis=1) - 1
    padded = ((csum[-1] + tm_moe - 1) // tm_moe) * tm_moe
    ends = jnp.cumsum(padded)
    dest = jnp.sum((ends - padded)[None, :] * onehot, axis=1) + rank
    n_tiles = pl.cdiv(n_rows, tm_moe) + N_GROUPS
    p_rows = n_tiles * tm_moe
    n_used = (ends[-1:] // tm_moe).astype(jnp.int32)
    tile_start = jnp.arange(n_tiles, dtype=jnp.int32) * tm_moe
    tile_group = jnp.minimum(jnp.sum((tile_start[:, None] >= ends[None, :]).astype(jnp.int32), axis=1), N_GROUPS - 1)
    src = jnp.zeros((p_rows,), jnp.int32).at[dest].set(jnp.arange(n_rows, dtype=jnp.int32), unique_indices=True)
    gate_sorted = jnp.zeros((p_rows, N_EXPERTS), f32).at[dest].set(gate, unique_indices=True)
    y_sorted = _moe_sorted(xf, src, gate_sorted, tile_group, n_used, wg, wu, wd, tm_moe)
    dest_pad = jnp.pad(dest, (0, pl.cdiv(n_rows, tm_ln) * tm_ln - n_rows))
    xo, xbo = _unpermute_ln(y_sorted, dest_pad, xf, ln_g, ln_b, tm_ln)
    return xo.reshape(shp), xbo.reshape(shp)


def _mlstm_proj_kernel(x_ref, w_ref, wgh_ref, wgl_ref, wgth_ref, wgtl_ref, z_ref, gcol_ref, grow_ref, *, seq_len, tm):
    i = pl.program_id(1)
    rows = i * tm + lax.broadcasted_iota(jnp.int32, (tm, 1), 0)
    x = x_ref[0]
    x = jnp.where(rows < seq_len, x, jnp.zeros_like(x))
    for part in range(3):
        sl = slice(part * D_MODEL, (part + 1) * D_MODEL)
        z_ref[0, :, sl] = jnp.dot(x, w_ref[:, sl], preferred_element_type=f32).astype(bf16)
    gcol_ref[0] = (jnp.dot(x, wgh_ref[...], preferred_element_type=f32)
                   + jnp.dot(x, wgl_ref[...], preferred_element_type=f32))
    grow_ref[0] = (lax.dot_general(wgth_ref[...], x, NT_DIMS, preferred_element_type=f32)
                   + lax.dot_general(wgtl_ref[...], x, NT_DIMS, preferred_element_type=f32))


def _mlstm_proj(x_b16, w_main, wg_cols, wg_rows, tm, chunk):
    nb, seq_len, _ = x_b16.shape
    lp = pl.cdiv(seq_len, chunk) * chunk
    tm = min(tm, lp)
    nt = pl.cdiv(lp, tm)
    const = lambda shape: pl.BlockSpec(shape, lambda b, i: (0,) * len(shape))
    return pl.pallas_call(
        functools.partial(_mlstm_proj_kernel, seq_len=seq_len, tm=tm),
        out_shape=(jax.ShapeDtypeStruct((nb, lp, 3 * D_MODEL), bf16),
                   jax.ShapeDtypeStruct((nb, lp, LANES), f32),
                   jax.ShapeDtypeStruct((nb, 2 * HEADS, lp), f32)),
        grid=(nb, nt),
        in_specs=[pl.BlockSpec((1, tm, D_MODEL), lambda b, i: (b, i, 0)),
                  const((D_MODEL, 3 * D_MODEL)), const((D_MODEL, LANES)), const((D_MODEL, LANES)),
                  const((2 * HEADS, D_MODEL)), const((2 * HEADS, D_MODEL))],
        out_specs=(pl.BlockSpec((1, tm, 3 * D_MODEL), lambda b, i: (b, i, 0)),
                   pl.BlockSpec((1, tm, LANES), lambda b, i: (b, i, 0)),
                   pl.BlockSpec((1, 2 * HEADS, tm), lambda b, i: (b, 0, i))),
        compiler_params=_params(("parallel", "parallel"), 48),
        name="mlstm_in_proj",
    )(x_b16, w_main, *wg_cols, *wg_rows)


def _softcap(z):
    return GATE_SOFTCAP * jnp.tanh(z / GATE_SOFTCAP)


def _log_sigmoid(z):
    return jnp.minimum(z, 0.0) - jnp.log(1.0 + jnp.exp(-jnp.abs(z)))


def _mlstm_chunk_kernel(z_ref, gcol_ref, grow_ref, bcol_ref, brow_ref, g_ref, c0_ref, n0_ref, m0_ref,
                        h_ref, c_ref, n_ref, m_ref, c_sc, n_sc, m_sc, *, seq_len, tc):
    ci = pl.program_id(1)

    @pl.when(ci == 0)
    def _():
        c_sc[...] = c0_ref[0]
        n_sc[...] = n0_ref[0]
        m_sc[...] = m0_ref[0]

    t_col = ci * tc + lax.broadcasted_iota(jnp.int32, (tc, LANES), 0)
    lane = lax.broadcasted_iota(jnp.int32, (tc, LANES), 1)
    pre = _softcap(gcol_ref[0] + bcol_ref[...])
    gates_c = jnp.where(lane < HEADS, pre, _log_sigmoid(pre))
    neutral_c = jnp.where(lane < HEADS, NEG_BIG, 0.0)
    gates_c = jnp.where(t_col < seq_len, gates_c, neutral_c)
    t_row = ci * tc + lax.broadcasted_iota(jnp.int32, (2 * HEADS, tc), 1)
    sub = lax.broadcasted_iota(jnp.int32, (2 * HEADS, tc), 0)
    pre_r = _softcap(grow_ref[0] + brow_ref[...])
    gates_r = jnp.where(sub < HEADS, pre_r, _log_sigmoid(pre_r))
    neutral_r = jnp.where(sub < HEADS, NEG_BIG, 0.0)
    gates_r = jnp.where(t_row < seq_len, gates_r, neutral_r)

    ri = lax.broadcasted_iota(jnp.int32, (tc, tc), 0)
    cj = lax.broadcasted_iota(jnp.int32, (tc, tc), 1)
    causal = cj <= ri
    tril = jnp.where(causal, 1.0, 0.0).astype(bf16)
    triu = jnp.where(ri <= cj, 1.0, 0.0).astype(bf16)
    hi, lo = _split_hi_lo(gates_c)
    cum_c = jnp.dot(tril, hi, preferred_element_type=f32) + jnp.dot(tril, lo, preferred_element_type=f32)
    hi, lo = _split_hi_lo(gates_r)
    cum_r = jnp.dot(hi, triu, preferred_element_type=f32) + jnp.dot(lo, triu, preferred_element_type=f32)

    k_scale = M_DQK ** -0.5
    for h in range(HEADS):
        q = z_ref[0, :, h * M_DQK:(h + 1) * M_DQK]
        k = z_ref[0, :, HEADS * M_DQK + h * M_DQK:HEADS * M_DQK + (h + 1) * M_DQK]
        v = z_ref[0, :, D_MODEL + h * M_DV:D_MODEL + (h + 1) * M_DV]
        og = z_ref[0, :, 2 * D_MODEL + h * M_DV:2 * D_MODEL + (h + 1) * M_DV]
        b_col = cum_c[:, HEADS + h:HEADS + h + 1]
        a_col = gates_c[:, h:h + 1] - b_col
        a_row = gates_r[h:h + 1, :] - cum_r[HEADS + h:HEADS + h + 1, :]
        m0 = m_sc[h:h + 1, 0:1]
        c0 = c_sc[h]
        n0 = n_sc[h:h + 1, :]

        amat = jnp.where(causal, a_row, NEG_BIG)
        u = jnp.maximum(m0, jnp.max(amat, axis=-1, keepdims=True))
        dmat = jnp.exp(amat - u)
        s = lax.dot_general(q, k, NT_DIMS, preferred_element_type=f32) * k_scale
        w = s * dmat
        g_in = jnp.exp(m0 - u)
        num = (jnp.dot(w.astype(bf16), v, preferred_element_type=f32)
               + g_in * jnp.dot(q, c0.astype(bf16), preferred_element_type=f32))
        qn = jnp.sum(q.astype(f32) * n0, axis=-1, keepdims=True)
        den = jnp.sum(w, axis=-1, keepdims=True) + g_in * qn
        m_tok = b_col + u
        hval = num / jnp.maximum(jnp.abs(den), jnp.exp(-m_tok))
        hval = hval * lax.rsqrt(jnp.mean(hval * hval, axis=-1, keepdims=True) + RMS_EPS)
        hval = hval * g_ref[:, h * M_DV:(h + 1) * M_DV] * jax.nn.sigmoid(og.astype(f32))
        h_ref[0, :, h * M_DV:(h + 1) * M_DV] = hval.astype(bf16)

        u_last = u[tc - 1:tc, :]
        g0 = jnp.exp(m0 - u_last)
        decay = jnp.exp(a_col - u_last)
        kd = k.astype(f32) * (decay * k_scale)
        c_sc[h] = g0 * c0 + lax.dot_general(kd.astype(bf16), v, TN_DIMS, preferred_element_type=f32)
        n_sc[h:h + 1, :] = g0 * n0 + jnp.sum(kd, axis=0, keepdims=True)
        m_sc[h:h + 1, :] = jnp.broadcast_to(b_col[tc - 1:tc, :] + u_last, (1, LANES))

    @pl.when(ci == pl.num_programs(1) - 1)
    def _():
        c_ref[0] = c_sc[...]
        n_ref[0] = n_sc[...]
        m_ref[0] = m_sc[...]


def _mlstm_chunks(z, gcol, grow, bcol, brow, norm_g, c0t, n0, m0b, seq_len, tc):
    nb, lp, _ = z.shape
    nc = lp // tc
    per_b = lambda shape: pl.BlockSpec((1,) + shape, lambda b, c: (b,) + (0,) * len(shape))
    const = lambda shape: pl.BlockSpec(shape, lambda b, c: (0,) * len(shape))
    state_specs = (per_b((HEADS, M_DQK, M_DV)), per_b((HEADS, M_DQK)), per_b((HEADS, LANES)))
    return pl.pallas_call(
        functools.partial(_mlstm_chunk_kernel, seq_len=seq_len, tc=tc),
        out_shape=(jax.ShapeDtypeStruct((nb, lp, D_MODEL), bf16),
                   jax.ShapeDtypeStruct((nb, HEADS, M_DQK, M_DV), f32),
                   jax.ShapeDtypeStruct((nb, HEADS, M_DQK), f32),
                   jax.ShapeDtypeStruct((nb, HEADS, LANES), f32)),
        grid=(nb, nc),
        in_specs=[pl.BlockSpec((1, tc, 3 * D_MODEL), lambda b, c: (b, c, 0)),
                  pl.BlockSpec((1, tc, LANES), lambda b, c: (b, c, 0)),
                  pl.BlockSpec((1, 2 * HEADS, tc), lambda b, c: (b, 0, c)),
                  const((1, LANES)), const((2 * HEADS, 1)), const((1, D_MODEL))] + list(state_specs),
        out_specs=(pl.BlockSpec((1, tc, D_MODEL), lambda b, c: (b, c, 0)),) + state_specs,
        scratch_shapes=[pltpu.VMEM((HEADS, M_DQK, M_DV), f32), pltpu.VMEM((HEADS, M_DQK), f32),
                        pltpu.VMEM((HEADS, LANES), f32)],
        compiler_params=_params(("parallel", "arbitrary"), 32),
        name="mlstm_chunks",
    )(z, gcol, grow, bcol, brow, norm_g, c0t, n0, m0b)


ATT_TILE = 512
SAMPLE_TILE = 128
MLSTM_CHUNK = 128
MLSTM_PROJ_TILE = 512
PAGES_PER_STEP = 8
MOE_TILE = 1024
MOE_LN_TILE = 512
ROUTER_TILE = 512


def _attention_layer(xp, xp_b, xs, cache_k, cache_v, layer, page_table, w_in, lam_p, g, w_out, ln_g, ln_b, lam_init):
    nbd, n_new, _ = xs.shape
    seq_len = xp.shape[1]
    past = page_table.shape[1] * cache_k.shape[2]
    qb, kb, vb, kf, vf = _qkv_rope(xp_b, w_in.astype(bf16), jnp.arange(seq_len, dtype=jnp.int32), ATT_TILE)
    o = _flash_attention(qb, kb, vb, lam_p, g, lam_init, ATT_TILE)
    xp1, xp1_b = _proj_ln(o, xp, w_out.astype(bf16), ln_g, ln_b, ATT_TILE)
    pos_s = past + (jnp.arange(nbd * n_new, dtype=jnp.int32) % n_new)
    xs_rows = xs.reshape(1, nbd * n_new, D_MODEL)
    _, _, _, ksf, vsf, qs = _qkv_rope(xs_rows, w_in, pos_s, SAMPLE_TILE, q_f32=True)
    qs = qs.reshape(nbd, n_new, D_MODEL)
    ksf = ksf.reshape(nbd, n_new, D_MODEL)
    vsf = vsf.reshape(nbd, n_new, D_MODEL)
    n_pool, page = cache_k.shape[1], cache_k.shape[2]
    ck = cache_k.reshape(-1, page * HEADS, HEAD_W)
    cv = cache_v.reshape(-1, page * HEADS, HEAD_W)
    os_ = _decode_attention(qs, ksf, vsf, ck, cv, page_table, layer * n_pool, lam_p, g, lam_init, PAGES_PER_STEP)
    os_rows = os_.reshape(1, nbd * n_new, D_MODEL)
    xs1, xs1_b = _proj_ln(os_rows, xs_rows, w_out, ln_g, ln_b, SAMPLE_TILE)
    return (xp1, xp1_b, xs1.reshape(xs.shape), xs1_b.reshape(xs.shape),
            kf.reshape(kf.shape[:2] + (HEADS, HEAD_W)), vf.reshape(vf.shape[:2] + (HEADS, HEAD_W)),
            ksf.reshape(nbd, n_new, HEADS, HEAD_W), vsf.reshape(nbd, n_new, HEADS, HEAD_W))


def _mlstm_stream(x, x_b, w_main, wg_cols, wg_rows, bcol, brow, norm_g, c0t, n0, m0b, w_out, ln_g, ln_b):
    nb, seq_len, _ = x.shape
    short = seq_len < MLSTM_CHUNK
    if short:
        rows = nb * seq_len
        z, gcol, grow = _mlstm_proj(x_b.reshape(1, rows, D_MODEL), w_main, wg_cols, wg_rows, rows, rows)
        pad = MLSTM_CHUNK - seq_len
        z = jnp.pad(z[0, :rows].reshape(nb, seq_len, 3 * D_MODEL), [(0, 0), (0, pad), (0, 0)])
        gcol = jnp.pad(gcol[0, :rows].reshape(nb, seq_len, LANES), [(0, 0), (0, pad), (0, 0)])
        grow = jnp.transpose(grow[0, :, :rows].reshape(2 * HEADS, nb, seq_len), (1, 0, 2))
        grow = jnp.pad(grow, [(0, 0), (0, 0), (0, pad)])
    else:
        z, gcol, grow = _mlstm_proj(x_b, w_main, wg_cols, wg_rows, MLSTM_PROJ_TILE, MLSTM_CHUNK)
    hb, ct, n, mb = _mlstm_chunks(z, gcol, grow, bcol, brow, norm_g, c0t, n0, m0b, seq_len, MLSTM_CHUNK)
    if short:
        hb = hb[:, :seq_len].reshape(1, rows, D_MODEL)
        x1, x1_b = _proj_ln(hb, x.reshape(1, rows, D_MODEL), w_out, ln_g, ln_b, rows)
        x1, x1_b = x1.reshape(x.shape), x1_b.reshape(x.shape)
    else:
        x1, x1_b = _proj_ln(hb, x, w_out, ln_g, ln_b, ATT_TILE)
    return x1, x1_b, jnp.swapaxes(ct, -1, -2), n, mb[..., 0]


def kernel(x_prompt, x_sample, cache_k, cache_v, state_C, state_n, state_m, page_table, meta_tokens, w_attn_in, lambda_q1, lambda_k1, lambda_q2, lambda_k2, subln_g, w_attn_out, w_mlstm_in, b_mlstm_if, mlstm_norm_g, w_mlstm_out, w_router, b_router, w_exp_gate, w_exp_up, w_exp_down, ln_g, ln_b):
    nb = x_prompt.shape[0]
    nbd = x_sample.shape[0]
    meta = jnp.broadcast_to(meta_tokens.astype(x_prompt.dtype)[None], (nb, N_META, D_MODEL))
    xp = jnp.concatenate([meta, x_prompt], axis=1)
    xs = x_sample
    xp_b = xp.astype(bf16)
    router_w = _split_hi_lo(w_router.T) + (b_router.astype(f32).reshape(N_EXPERTS, 1),)
    prompt_moe_tile = min(MOE_TILE, xp.shape[0] * xp.shape[1])
    sample_rows = nbd * xs.shape[1]

    def moe_pair(i, xp, xs):
        args = (router_w, w_exp_gate[i], w_exp_up[i], w_exp_down[i], ln_g[i, 1:2], ln_b[i, 1:2])
        xp, xp_b = _moe_block(xp, *args, ROUTER_TILE, prompt_moe_tile, min(MOE_LN_TILE, prompt_moe_tile))
        xs, xs_b = _moe_block(xs, *args, sample_rows, sample_rows, sample_rows)
        return xp, xp_b, xs, xs_b

    lam_init = 0.8 - 0.6 * math.exp(-0.3 * 0)
    lam_p = jnp.stack([lambda_q1[0], lambda_k1[0], lambda_q2[0], lambda_k2[0]]).astype(f32)
    (xp, xp_b, xs, xs_b, k_p, v_p, k_s, v_s) = _attention_layer(
        xp, xp_b, xs, cache_k, cache_v, 0, page_table, w_attn_in[0], lam_p,
        subln_g[0].reshape(1, HEAD_W), w_attn_out[0], ln_g[0, 0:1], ln_b[0, 0:1], lam_init)
    xp, xp_b, xs, xs_b = moe_pair(0, xp, xs)

    w_in = w_mlstm_in[0]
    w_main = w_in[:, :3 * D_MODEL].astype(bf16)
    w_gate = w_in[:, 3 * D_MODEL:]
    wg_cols = _split_hi_lo(jnp.pad(w_gate, [(0, 0), (0, LANES - 2 * HEADS)]))
    wg_rows = _split_hi_lo(w_gate.T)
    b_if = b_mlstm_if[0].astype(f32).reshape(2 * HEADS)
    bcol = jnp.pad(b_if, (0, LANES - 2 * HEADS)).reshape(1, LANES)
    brow = b_if.reshape(2 * HEADS, 1)
    norm_g = mlstm_norm_g[0].reshape(1, D_MODEL)
    common = (w_main, wg_cols, wg_rows, bcol, brow, norm_g)
    tail = (w_mlstm_out[0].astype(bf16), ln_g[1, 0:1], ln_b[1, 0:1])
    zero_state = (jnp.zeros((nb, HEADS, M_DQK, M_DV), f32), jnp.zeros((nb, HEADS, M_DQK), f32),
                  jnp.zeros((nb, HEADS, LANES), f32))
    xp, xp_b, c_p, n_p, m_p = _mlstm_stream(xp, xp_b, *common, *zero_state, *tail)
    sample_state = (jnp.swapaxes(state_C[0].astype(f32), -1, -2), state_n[0].astype(f32),
                    jnp.broadcast_to(state_m[0].astype(f32)[..., None], (nbd, HEADS, LANES)))
    xs, xs_b, c_s, n_s, m_s = _mlstm_stream(xs, xs_b, *common, *sample_state, *tail)
    xp, xp_b, xs, xs_b = moe_pair(1, xp, xs)

    return (xp[:, N_META:], xs, k_p[None], v_p[None], k_s[None], v_s[None],
            c_p[None], n_p[None], m_p[None], c_s[None], n_s[None], m_s[None])
```

```python
import functools
import math

import jax
import jax.numpy as jnp
from jax import lax
from jax.experimental import pallas as pl
from jax.experimental.pallas import tpu as pltpu

f32 = jnp.float32
bf16 = jnp.bfloat16

D_MODEL = 1024
N_META = 16
HEADS = 8
HEAD_W = 128
SUB_W = 64
ATT_SCALE = SUB_W ** -0.5
Q_SCALE = ATT_SCALE * math.log2(math.e)
ROT_DIM = 16
ROPE_THETA = 500000.0
M_DQK = 64
M_DV = 128
GATE_SOFTCAP = 15.0
N_EXPERTS = 16
N_GROUPS = 4
EXPERTS_PER_GROUP = 4
D_EXPERT = 512
DEPTH = 2
DN_ALPHA = (2 * DEPTH) ** 0.25
LN_EPS = 1e-5
RMS_EPS = 1e-6
NEG_BIG = -1e30

LANES = 128
MIB = 1024 * 1024

NT_DIMS = (((1,), (1,)), ((), ()))
TN_DIMS = (((0,), (0,)), ((), ()))


def _params(semantics, vmem_mib):
    return pltpu.CompilerParams(dimension_semantics=semantics, vmem_limit_bytes=vmem_mib * MIB)


def _split_hi_lo(a):
    hi = a.astype(bf16)
    lo = (a - hi.astype(f32)).astype(bf16)
    return hi, lo


def _split_hi_lo_glue(a):
    bits = lax.bitcast_convert_type(a, jnp.uint32) & jnp.uint32(0xFFFF0000)
    hi = lax.bitcast_convert_type(bits, f32)
    return hi.astype(bf16), (a - hi).astype(bf16)


def _matmul(x, w):
    if x.dtype == bf16:
        return jnp.dot(x, w, preferred_element_type=f32)
    xh, xl = _split_hi_lo(x)
    wh, wl = _split_hi_lo(w)
    return (jnp.dot(xh, wh, preferred_element_type=f32)
            + (jnp.dot(xl, wh, preferred_element_type=f32) + jnp.dot(xh, wl, preferred_element_type=f32)))


def _layer_norm(r, g, b):
    mu = jnp.mean(r, axis=-1, keepdims=True)
    rc = r - mu
    var = jnp.mean(rc * rc, axis=-1, keepdims=True)
    return rc * lax.rsqrt(var + LN_EPS) * g + b


def _qkv_rope_kernel(x_ref, w_ref, c_ref, s1_ref, s2_ref,
                     qb_ref, kb_ref, vb_ref, kf_ref, vf_ref, *maybe_qf_ref, seq_len, tm):
    i = pl.program_id(1)
    rows = i * tm + lax.broadcasted_iota(jnp.int32, (tm, 1), 0)
    valid = rows < seq_len
    x = x_ref[0]
    c = c_ref[...]
    s1 = s1_ref[...]
    s2 = s2_ref[...]
    for part in range(3):
        y = _matmul(x, w_ref[:, part * D_MODEL:(part + 1) * D_MODEL])
        for h in range(HEADS):
            sl = slice(h * HEAD_W, (h + 1) * HEAD_W)
            yh = y[:, sl]
            if part < 2:
                yh = yh * c + pltpu.roll(yh, LANES - ROT_DIM // 2, 1) * s1 + pltpu.roll(yh, ROT_DIM // 2, 1) * s2
            if part == 0:
                qb_ref[0, :, sl] = jnp.where(valid, yh * Q_SCALE, 0.0).astype(bf16)
                for qf_ref in maybe_qf_ref:
                    qf_ref[0, :, sl] = yh * Q_SCALE
            elif part == 1:
                kf_ref[0, :, sl] = yh
                kb_ref[0, :, sl] = jnp.where(valid, yh, 0.0).astype(bf16)
            else:
                vf_ref[0, :, sl] = yh
                vb_ref[0, :, sl] = jnp.where(valid, yh, 0.0).astype(bf16)


def _rope_tables(pos):
    half = ROT_DIM // 2
    inv_freq = ROPE_THETA ** (-jnp.arange(half, dtype=f32) * (2.0 / ROT_DIM))
    ang = pos.astype(f32)[:, None] * inv_freq[None, :]
    cos = jnp.cos(ang)
    sin = jnp.sin(ang)
    t = pos.shape[0]
    ones = jnp.ones((t, SUB_W - ROT_DIM), f32)
    zeros8 = jnp.zeros((t, half), f32)
    zeros48 = jnp.zeros((t, SUB_W - ROT_DIM), f32)
    c = jnp.concatenate([cos, cos, ones], axis=1)
    s1 = jnp.concatenate([-sin, zeros8, zeros48], axis=1)
    s2 = jnp.concatenate([zeros8, sin, zeros48], axis=1)
    return tuple(jnp.concatenate([a, a], axis=1) for a in (c, s1, s2))


def _qkv_rope(x, w, pos, tm, q_f32=False):
    nb, seq_len, _ = x.shape
    nt = pl.cdiv(seq_len, tm)
    lp = nt * tm
    c, s1, s2 = _rope_tables(pos)
    row_spec = pl.BlockSpec((1, tm, D_MODEL), lambda b, i: (b, i, 0))
    tab_spec = pl.BlockSpec((tm, HEAD_W), lambda b, i: (i, 0))
    pad_shape = jax.ShapeDtypeStruct((nb, lp, D_MODEL), bf16)
    out_shape = jax.ShapeDtypeStruct((nb, seq_len, D_MODEL), f32)
    n_out = 6 if q_f32 else 5
    return pl.pallas_call(
        functools.partial(_qkv_rope_kernel, seq_len=seq_len, tm=tm),
        out_shape=(pad_shape, pad_shape, pad_shape) + (out_shape,) * (n_out - 3),
        grid=(nb, nt),
        in_specs=[row_spec, pl.BlockSpec((D_MODEL, 3 * D_MODEL), lambda b, i: (0, 0)), tab_spec, tab_spec, tab_spec],
        out_specs=(row_spec,) * n_out,
        compiler_params=_params(("parallel", "parallel"), 48),
        name="qkv_rope",
    )(x, w, c, s1, s2)


def _diff_lambda(lam_ref, lam_init):
    p = lam_ref[...]
    l1 = jnp.sum(p[0:1] * p[1:2], axis=-1, keepdims=True)
    l2 = jnp.sum(p[2:3] * p[3:4], axis=-1, keepdims=True)
    return jnp.exp(l1) - jnp.exp(l2) + lam_init


def _diff_finish(o1, o2, lam, g, lam_init):
    o = o1 - lam * o2
    o = o * lax.rsqrt(jnp.mean(o * o, axis=-1, keepdims=True) + RMS_EPS)
    return o * g * (1.0 - lam_init)


def _flash_kernel(lam_ref, g_ref, q_ref, k_ref, v_ref, o_ref, qq_sc, m_sc, l_sc, acc_sc, *, lam_init, tq):
    qi = pl.program_id(2)
    q = q_ref[0]
    lane = lax.broadcasted_iota(jnp.int32, (tq, HEAD_W), 1)
    zero = jnp.zeros_like(q)
    qq_sc[0:tq, :] = jnp.where(lane < SUB_W, q, zero)
    qq_sc[tq:2 * tq, :] = jnp.where(lane >= SUB_W, q, zero)
    m_sc[...] = jnp.full(m_sc.shape, NEG_BIG, f32)
    l_sc[...] = jnp.zeros(l_sc.shape, f32)
    acc_sc[...] = jnp.zeros(acc_sc.shape, f32)

    def step(kstart, masked):
        k = k_ref[0, pl.ds(kstart, tq), :]
        v = v_ref[0, pl.ds(kstart, tq), :]
        s = lax.dot_general(qq_sc[...], k, NT_DIMS, preferred_element_type=f32)
        if masked:
            r = lax.broadcasted_iota(jnp.int32, (2 * tq, tq), 0)
            r = jnp.where(r >= tq, r - tq, r)
            cidx = lax.broadcasted_iota(jnp.int32, (2 * tq, tq), 1)
            s = jnp.where(cidx <= r, s, NEG_BIG)
        m_prev = m_sc[...]
        m_new = jnp.maximum(m_prev, jnp.max(s, axis=-1, keepdims=True))
        alpha = jnp.exp2(m_prev - m_new)
        parts = [jnp.exp2(s[:, j * LANES:(j + 1) * LANES] - m_new) for j in range(tq // LANES)]
        psum = parts[0]
        for part in parts[1:]:
            psum = psum + part
        p = jnp.concatenate([part.astype(bf16) for part in parts], axis=1)
        l_sc[...] = alpha * l_sc[...] + psum
        acc_sc[...] = alpha * acc_sc[...] + jnp.dot(p, v, preferred_element_type=f32)
        m_sc[...] = m_new

    def body(ki, carry):
        step(pl.multiple_of(ki * tq, tq), False)
        return carry

    lax.fori_loop(0, qi, body, 0)
    step(pl.multiple_of(qi * tq, tq), True)

    lam = _diff_lambda(lam_ref, lam_init)
    l = jnp.sum(l_sc[...], axis=-1, keepdims=True)
    o1 = acc_sc[0:tq, :] / l[0:tq, :]
    o2 = acc_sc[tq:2 * tq, :] / l[tq:2 * tq, :]
    o_ref[0] = _diff_finish(o1, o2, lam, g_ref[...], lam_init).astype(bf16)


def _flash_attention(qb, kb, vb, lam_p, g, lam_init, tq):
    nb, lp, _ = qb.shape
    nq = lp // tq
    q_spec = pl.BlockSpec((1, tq, HEAD_W), lambda b, h, i: (b, i, h))
    kv_spec = pl.BlockSpec((1, lp, HEAD_W), lambda b, h, i: (b, 0, h))
    return pl.pallas_call(
        functools.partial(_flash_kernel, lam_init=lam_init, tq=tq),
        out_shape=jax.ShapeDtypeStruct((nb, lp, D_MODEL), bf16),
        grid=(nb, HEADS, nq),
        in_specs=[pl.BlockSpec((4, SUB_W), lambda b, h, i: (0, 0)),
                  pl.BlockSpec((1, HEAD_W), lambda b, h, i: (0, 0)),
                  q_spec, kv_spec, kv_spec],
        out_specs=q_spec,
        scratch_shapes=[pltpu.VMEM((2 * tq, HEAD_W), bf16), pltpu.VMEM((2 * tq, LANES), f32),
                        pltpu.VMEM((2 * tq, LANES), f32), pltpu.VMEM((2 * tq, HEAD_W), f32)],
        compiler_params=_params(("parallel", "parallel", "arbitrary"), 48),
        name="flash_diff_attention",
    )(lam_p, g, qb, kb, vb)


def _decode_kernel(pt_ref, lam_ref, g_ref, q_ref, kn_ref, vn_ref, *rest, lam_init, pages_per_step, n_new):
    k_refs = rest[:pages_per_step]
    v_refs = rest[pages_per_step:2 * pages_per_step]
    o_ref = rest[2 * pages_per_step]
    m_sc, l_sc, acc_sc = rest[2 * pages_per_step + 1:]
    del pt_ref
    p_idx = pl.program_id(1)
    rph = 2 * n_new
    n_rows = HEADS * rph
    page = k_refs[0].shape[1] // HEADS

    @pl.when(p_idx == 0)
    def _():
        m_sc[...] = jnp.full(m_sc.shape, NEG_BIG, f32)
        l_sc[...] = jnp.zeros(l_sc.shape, f32)
        acc_sc[...] = jnp.zeros(acc_sc.shape, f32)

    def head_rows(ref, h):
        return _split_hi_lo(ref[0, pl.ds(h, page, stride=HEADS), :])

    def fold(a):
        return a[0:rph] + a[rph:2 * rph]

    def step(kv_refs, masked):
        scores = []
        for k_ref, _ in kv_refs:
            per_head = []
            for h in range(HEADS):
                q2 = q_ref[0, h]
                k_hi, k_lo = head_rows(k_ref, h)
                a = (lax.dot_general(q2, k_hi, NT_DIMS, preferred_element_type=f32)
                     + lax.dot_general(q2, k_lo, NT_DIMS, preferred_element_type=f32))
                per_head.append(fold(a))
            s = jnp.concatenate(per_head, axis=0)
            if masked:
                r = lax.broadcasted_iota(jnp.int32, (n_rows, page), 0)
                tok = r % n_new
                cidx = lax.broadcasted_iota(jnp.int32, (n_rows, page), 1)
                s = jnp.where(cidx <= tok, s, NEG_BIG)
            scores.append(s)
        m_prev = m_sc[...]
        m_new = m_prev
        for s in scores:
            m_new = jnp.maximum(m_new, jnp.max(s, axis=-1, keepdims=True))
        alpha = jnp.exp2(m_prev - m_new)
        l_new = alpha * l_sc[...]
        pv = [None] * HEADS
        for s, (_, v_ref) in zip(scores, kv_refs):
            p = jnp.exp2(s - m_new)
            l_new = l_new + jnp.sum(p, axis=-1, keepdims=True)
            for h in range(HEADS):
                p2 = jnp.concatenate(_split_hi_lo(p[h * rph:(h + 1) * rph]), axis=0)
                v_hi, v_lo = head_rows(v_ref, h)
                part = fold(jnp.dot(p2, v_hi, preferred_element_type=f32)
                            + jnp.dot(p2, v_lo, preferred_element_type=f32))
                pv[h] = part if pv[h] is None else pv[h] + part
        l_sc[...] = l_new
        for h in range(HEADS):
            rs = slice(h * rph, (h + 1) * rph)
            acc_sc[rs, :] = alpha[rs, :] * acc_sc[rs, :] + pv[h]
        m_sc[...] = m_new

    step(list(zip(k_refs, v_refs)), False)

    @pl.when(p_idx == pl.num_programs(1) - 1)
    def _():
        step([(kn_ref, vn_ref)], True)
        lam = _diff_lambda(lam_ref, lam_init)
        o = acc_sc[...] / l_sc[...]
        for h in range(HEADS):
            o1 = o[h * rph:h * rph + n_new, :]
            o2 = o[h * rph + n_new:(h + 1) * rph, :]
            o_ref[0, :, h * HEAD_W:(h + 1) * HEAD_W] = _diff_finish(o1, o2, lam, g_ref[...], lam_init)


def _decode_attention(q, k_new, v_new, cache_k, cache_v, page_table, page_base, lam_p, g, lam_init,
                      pages_per_step):
    nbd, n_new, _ = q.shape
    rph = 2 * n_new
    assert rph % 8 == 0, "each head's (sub-head, token) rows must fill whole sublane tiles"
    page_rows = cache_k.shape[1]
    n_pages = page_table.shape[1]
    n_rows = HEADS * rph
    qh = q.reshape(nbd, n_new, HEADS, HEAD_W).transpose(0, 2, 1, 3)
    lane_sub = jnp.arange(HEAD_W, dtype=jnp.int32) // SUB_W
    q_sub = [jnp.where(lane_sub == c, qh, jnp.zeros_like(qh)) for c in range(2)]
    q2 = jnp.concatenate(_split_hi_lo_glue(jnp.concatenate(q_sub, axis=2)), axis=2)
    new_rows = lambda a: jnp.pad(a.reshape(nbd, n_new * HEADS, HEAD_W), [(0, 0), (0, page_rows - n_new * HEADS), (0, 0)])
    kn = new_rows(k_new)
    vn = new_rows(v_new)
    steps = n_pages // pages_per_step

    def page_spec(j):
        return pl.BlockSpec((1, page_rows, HEAD_W),
                            lambda b, p, pt: (page_base + pt[b, p * pages_per_step + j], 0, 0))

    per_seq = lambda shape: pl.BlockSpec((1,) + shape, lambda b, p, pt: (b,) + (0,) * len(shape))
    grid_spec = pltpu.PrefetchScalarGridSpec(
        num_scalar_prefetch=1,
        grid=(nbd, steps),
        in_specs=[pl.BlockSpec((4, SUB_W), lambda b, p, pt: (0, 0)),
                  pl.BlockSpec((1, HEAD_W), lambda b, p, pt: (0, 0)),
                  per_seq((HEADS, 2 * rph, HEAD_W)), per_seq((page_rows, HEAD_W)), per_seq((page_rows, HEAD_W))]
                 + [page_spec(j) for j in range(pages_per_step)]
                 + [page_spec(j) for j in range(pages_per_step)],
        out_specs=per_seq((n_new, D_MODEL)),
        scratch_shapes=[pltpu.VMEM((n_rows, 1), f32), pltpu.VMEM((n_rows, 1), f32),
                        pltpu.VMEM((n_rows, HEAD_W), f32)],
    )
    return pl.pallas_call(
        functools.partial(_decode_kernel, lam_init=lam_init, pages_per_step=pages_per_step, n_new=n_new),
        out_shape=jax.ShapeDtypeStruct((nbd, n_new, D_MODEL), f32),
        grid_spec=grid_spec,
        compiler_params=_params(("parallel", "arbitrary"), 40),
        name="paged_decode_attention",
    )(page_table, lam_p, g, q2, kn, vn, *([cache_k] * pages_per_step), *([cache_v] * pages_per_step))


def _proj_ln_kernel(o_ref, x_ref, w_ref, g_ref, b_ref, xo_ref, xb_ref):
    y = _matmul(o_ref[0], w_ref[...])
    out = _layer_norm(DN_ALPHA * x_ref[0] + y, g_ref[...], b_ref[...])
    xo_ref[0] = out
    xb_ref[0] = out.astype(bf16)


def _proj_ln(o, x, w, ln_g, ln_b, tm):
    nb, seq_len, _ = x.shape
    row_spec = pl.BlockSpec((1, tm, D_MODEL), lambda b, i: (b, i, 0))
    vec_spec = pl.BlockSpec((1, D_MODEL), lambda b, i: (0, 0))
    return pl.pallas_call(
        _proj_ln_kernel,
        out_shape=(jax.ShapeDtypeStruct(x.shape, f32), jax.ShapeDtypeStruct(x.shape, bf16)),
        grid=(nb, pl.cdiv(seq_len, tm)),
        in_specs=[row_spec, row_spec, pl.BlockSpec((D_MODEL, D_MODEL), lambda b, i: (0, 0)), vec_spec, vec_spec],
        out_specs=(row_spec, row_spec),
        compiler_params=_params(("parallel", "parallel"), 32),
        name="proj_residual_layernorm",
    )(o, x, w, ln_g, ln_b)


def _router_kernel(x_ref, wh_ref, wl_ref, b_ref, gate_ref):
    xh, xl = _split_hi_lo(x_ref[...])
    wh = wh_ref[...]
    logits = (lax.dot_general(wh, xh, NT_DIMS, preferred_element_type=f32)
              + (lax.dot_general(wh, xl, NT_DIMS, preferred_element_type=f32)
                 + lax.dot_general(wl_ref[...], xh, NT_DIMS, preferred_element_type=f32)))
    s = jax.nn.sigmoid(logits)
    sel = s + b_ref[...]
    rows = [sel[e:e + 1, :] for e in range(N_EXPERTS)]
    scores = []
    for gidx in range(N_GROUPS):
        r = rows[gidx * EXPERTS_PER_GROUP:(gidx + 1) * EXPERTS_PER_GROUP]
        best = None
        for a in range(EXPERTS_PER_GROUP):
            for b in range(a + 1, EXPERTS_PER_GROUP):
                pair = r[a] + r[b]
                best = pair if best is None else jnp.maximum(best, pair)
        scores.append(best)
    picked = []
    for e in range(N_EXPERTS):
        gidx, loc = divmod(e, EXPERTS_PER_GROUP)
        win = None
        for j in range(N_GROUPS):
            if j == gidx:
                continue
            c = scores[gidx] > scores[j] if j < gidx else scores[gidx] >= scores[j]
            win = c if win is None else jnp.logical_and(win, c)
        rank = jnp.zeros_like(rows[e])
        for j in range(EXPERTS_PER_GROUP):
            if j == loc:
                continue
            o = rows[gidx * EXPERTS_PER_GROUP + j]
            ahead = o >= rows[e] if j < loc else o > rows[e]
            rank = rank + ahead.astype(f32)
        chosen = jnp.logical_and(win, rank < 1.5)
        picked.append(jnp.where(chosen, s[e:e + 1, :], 0.0))
    total = picked[0]
    for e in range(1, N_EXPERTS):
        total = total + picked[e]
    for e in range(N_EXPERTS):
        gate_ref[e:e + 1, :] = picked[e] / total


def _router(x, w_hi, w_lo, b_col, tm):
    n_rows = x.shape[0]
    return pl.pallas_call(
        _router_kernel,
        out_shape=jax.ShapeDtypeStruct((N_EXPERTS, n_rows), f32),
        grid=(pl.cdiv(n_rows, tm),),
        in_specs=[pl.BlockSpec((tm, D_MODEL), lambda i: (i, 0)),
                  pl.BlockSpec((N_EXPERTS, D_MODEL), lambda i: (0, 0)),
                  pl.BlockSpec((N_EXPERTS, D_MODEL), lambda i: (0, 0)),
                  pl.BlockSpec((N_EXPERTS, 1), lambda i: (0, 0))],
        out_specs=pl.BlockSpec((N_EXPERTS, tm), lambda i: (0, i)),
        compiler_params=_params(("parallel",), 32),
        name="moe_router",
    )(x, w_hi, w_lo, b_col)


def _row_copy(src_hbm, idx_ref, base, r, dst, sem):
    return pltpu.make_async_copy(src_hbm.at[pl.ds(idx_ref[base + r], 1)], dst.at[pl.ds(r, 1)], sem)


def _gather_rows(op, src_hbm, idx_ref, base, dst, sem, n_rows):
    def body(r, carry):
        getattr(_row_copy(src_hbm, idx_ref, base, r, dst, sem), op)()
        return carry

    lax.fori_loop(0, n_rows, body, 0, unroll=8)


def _moe_sorted_kernel(tg_ref, nu_ref, src_ref, x_hbm, gate_ref, wg_ref, wu_ref, wd_ref, y_ref,
                       xbuf, sem, xb_sc, acc_sc, *, tm):
    i = pl.program_id(0)
    j = pl.program_id(1)
    n_used = nu_ref[0]
    slot = i % 2
    live = i < n_used

    @pl.when(j == 0)
    def _():
        @pl.when(i == 0)
        def _():
            _gather_rows("start", x_hbm, src_ref, 0, xbuf.at[0], sem.at[0], tm)

        @pl.when(i + 1 < n_used)
        def _():
            _gather_rows("start", x_hbm, src_ref, (i + 1) * tm, xbuf.at[1 - slot], sem.at[1 - slot], tm)

        @pl.when(live)
        def _():
            _gather_rows("wait", x_hbm, src_ref, i * tm, xbuf.at[slot], sem.at[slot], tm)
            xb_sc[...] = xbuf[slot].astype(bf16)
            acc_sc[...] = jnp.zeros(acc_sc.shape, f32)

    @pl.when(live)
    def _():
        e = tg_ref[i] * EXPERTS_PER_GROUP + j
        x = xb_sc[...]
        a = jnp.dot(x, wg_ref[0].astype(bf16), preferred_element_type=f32)
        u = jnp.dot(x, wu_ref[0].astype(bf16), preferred_element_type=f32)
        hmid = (a * jax.nn.sigmoid(a) * u).astype(bf16)
        y = jnp.dot(hmid, wd_ref[0].astype(bf16), preferred_element_type=f32)
        gate = gate_ref[...]
        lane = lax.broadcasted_iota(jnp.int32, gate.shape, 1)
        gcol = jnp.sum(jnp.where(lane == e, gate, 0.0), axis=-1, keepdims=True)
        acc_sc[...] += gcol * y

    last = j == EXPERTS_PER_GROUP - 1

    @pl.when(jnp.logical_and(last, live))
    def _():
        y_ref[...] = acc_sc[...]

    @pl.when(jnp.logical_and(last, jnp.logical_not(live)))
    def _():
        y_ref[...] = jnp.zeros(y_ref.shape, f32)


def _moe_sorted(x, src, gate_sorted, tile_group, n_used, wg, wu, wd, tm):
    p_rows = src.shape[0]
    n_tiles = p_rows // tm

    def w_spec(shape):
        def index(i, j, tg, nu, sr):
            return (tg[i] * EXPERTS_PER_GROUP + jnp.where(i < nu[0], j, EXPERTS_PER_GROUP - 1), 0, 0)
        return pl.BlockSpec((1,) + shape, index)

    grid_spec = pltpu.PrefetchScalarGridSpec(
        num_scalar_prefetch=3,
        grid=(n_tiles, EXPERTS_PER_GROUP),
        in_specs=[pl.BlockSpec(memory_space=pl.ANY),
                  pl.BlockSpec((tm, N_EXPERTS), lambda i, j, tg, nu, sr: (i, 0)),
                  w_spec((D_MODEL, D_EXPERT)), w_spec((D_MODEL, D_EXPERT)), w_spec((D_EXPERT, D_MODEL))],
        out_specs=pl.BlockSpec((tm, D_MODEL), lambda i, j, tg, nu, sr: (i, 0)),
        scratch_shapes=[pltpu.VMEM((2, tm, D_MODEL), f32), pltpu.SemaphoreType.DMA((2,)),
                        pltpu.VMEM((tm, D_MODEL), bf16), pltpu.VMEM((tm, D_MODEL), f32)],
    )
    return pl.pallas_call(
        functools.partial(_moe_sorted_kernel, tm=tm),
        out_shape=jax.ShapeDtypeStruct((p_rows, D_MODEL), f32),
        grid_spec=grid_spec,
        compiler_params=_params(("arbitrary", "arbitrary"), 56),
        name="moe_sorted_experts",
    )(tile_group, n_used, src, x, gate_sorted, wg, wu, wd)


def _unpermute_ln_kernel(dest_ref, y_hbm, x_ref, g_ref, b_ref, xo_ref, xbo_ref, ybuf, sem, *, tm):
    i = pl.program_id(0)
    slot = i % 2

    @pl.when(i == 0)
    def _():
        _gather_rows("start", y_hbm, dest_ref, 0, ybuf.at[0], sem.at[0], tm)

    @pl.when(i + 1 < pl.num_programs(0))
    def _():
        _gather_rows("start", y_hbm, dest_ref, (i + 1) * tm, ybuf.at[1 - slot], sem.at[1 - slot], tm)

    _gather_rows("wait", y_hbm, dest_ref, i * tm, ybuf.at[slot], sem.at[slot], tm)
    out = _layer_norm(DN_ALPHA * x_ref[...] + ybuf[slot], g_ref[...], b_ref[...])
    xo_ref[...] = out
    xbo_ref[...] = out.astype(bf16)


def _unpermute_ln(y_sorted, dest, x, ln_g, ln_b, tm):
    n_rows = x.shape[0]
    row_spec = pl.BlockSpec((tm, D_MODEL), lambda i, d: (i, 0))
    vec_spec = pl.BlockSpec((1, D_MODEL), lambda i, d: (0, 0))
    grid_spec = pltpu.PrefetchScalarGridSpec(
        num_scalar_prefetch=1,
        grid=(pl.cdiv(n_rows, tm),),
        in_specs=[pl.BlockSpec(memory_space=pl.ANY), row_spec, vec_spec, vec_spec],
        out_specs=(row_spec, row_spec),
        scratch_shapes=[pltpu.VMEM((2, tm, D_MODEL), f32), pltpu.SemaphoreType.DMA((2,))],
    )
    return pl.pallas_call(
        functools.partial(_unpermute_ln_kernel, tm=tm),
        out_shape=(jax.ShapeDtypeStruct(x.shape, f32), jax.ShapeDtypeStruct(x.shape, bf16)),
        grid_spec=grid_spec,
        compiler_params=_params(("arbitrary",), 40),
        name="moe_unpermute_layernorm",
    )(dest, y_sorted, x, ln_g, ln_b)


def _moe_block(x, router_w, wg, wu, wd, ln_g, ln_b, tm_router, tm_moe, tm_ln):
    shp = x.shape
    xf = x.reshape(-1, D_MODEL)
    n_rows = xf.shape[0]
    gate = _router(xf, *router_w, tm_router).T
    grp = (jnp.argmax(gate, axis=1) // EXPERTS_PER_GROUP).astype(jnp.int32)
    onehot = (grp[:, None] == jnp.arange(N_GROUPS, dtype=jnp.int32)[None, :]).astype(jnp.int32)
    csum = jnp.cumsum(onehot, axis=0)
    rank = jnp.sum(csum * onehot, axis=1) - 1
    padded = ((csum[-1] + tm_moe - 1) // tm_moe) * tm_moe
    ends = jnp.cumsum(padded)
    dest = jnp.sum((ends - padded)[None, :] * onehot, axis=1) + rank
    n_tiles = pl.cdiv(n_rows, tm_moe) + N_GROUPS
    p_rows = n_tiles * tm_moe
    n_used = (ends[-1:] // tm_moe).astype(jnp.int32)
    tile_start = jnp.arange(n_tiles, dtype=jnp.int32) * tm_moe
    tile_group = jnp.minimum(jnp.sum((tile_start[:, None] >= ends[None, :]).astype(jnp.int32), axis=1), N_GROUPS - 1)
    src = jnp.zeros((p_rows,), jnp.int32).at[dest].set(jnp.arange(n_rows, dtype=jnp.int32), unique_indices=True)
    gate_sorted = jnp.zeros((p_rows, N_EXPERTS), f32).at[dest].set(gate, unique_indices=True)
    y_sorted = _moe_sorted(xf, src, gate_sorted, tile_group, n_used, wg, wu, wd, tm_moe)
    dest_pad = jnp.pad(dest, (0, pl.cdiv(n_rows, tm_ln) * tm_ln - n_rows))
    xo, xbo = _unpermute_ln(y_sorted, dest_pad, xf, ln_g, ln_b, tm_ln)
    return xo.reshape(shp), xbo.reshape(shp)


def _mlstm_proj_kernel(x_ref, w_ref, wgh_ref, wgl_ref, wgth_ref, wgtl_ref, z_ref, gcol_ref, grow_ref, *, seq_len, tm):
    i = pl.program_id(1)
    rows = i * tm + lax.broadcasted_iota(jnp.int32, (tm, 1), 0)
    x = x_ref[0]
    x = jnp.where(rows < seq_len, x, jnp.zeros_like(x))
    for part in range(3):
        sl = slice(part * D_MODEL, (part + 1) * D_MODEL)
        z_ref[0, :, sl] = jnp.dot(x, w_ref[:, sl], preferred_element_type=f32).astype(bf16)
    gcol_ref[0] = (jnp.dot(x, wgh_ref[...], preferred_element_type=f32)
                   + jnp.dot(x, wgl_ref[...], preferred_element_type=f32))
    grow_ref[0] = (lax.dot_general(wgth_ref[...], x, NT_DIMS, preferred_element_type=f32)
                   + lax.dot_general(wgtl_ref[...], x, NT_DIMS, preferred_element_type=f32))


def _mlstm_proj(x_b16, w_main, wg_cols, wg_rows, tm, chunk):
    nb, seq_len, _ = x_b16.shape
    lp = pl.cdiv(seq_len, chunk) * chunk
    tm = min(tm, lp)
    nt = pl.cdiv(lp, tm)
    const = lambda shape: pl.BlockSpec(shape, lambda b, i: (0,) * len(shape))
    return pl.pallas_call(
        functools.partial(_mlstm_proj_kernel, seq_len=seq_len, tm=tm),
        out_shape=(jax.ShapeDtypeStruct((nb, lp, 3 * D_MODEL), bf16),
                   jax.ShapeDtypeStruct((nb, lp, LANES), f32),
                   jax.ShapeDtypeStruct((nb, 2 * HEADS, lp), f32)),
        grid=(nb, nt),
        in_specs=[pl.BlockSpec((1, tm, D_MODEL), lambda b, i: (b, i, 0)),
                  const((D_MODEL, 3 * D_MODEL)), const((D_MODEL, LANES)), const((D_MODEL, LANES)),
                  const((2 * HEADS, D_MODEL)), const((2 * HEADS, D_MODEL))],
        out_specs=(pl.BlockSpec((1, tm, 3 * D_MODEL), lambda b, i: (b, i, 0)),
                   pl.BlockSpec((1, tm, LANES), lambda b, i: (b, i, 0)),
                   pl.BlockSpec((1, 2 * HEADS, tm), lambda b, i: (b, 0, i))),
        compiler_params=_params(("parallel", "parallel"), 48),
        name="mlstm_in_proj",
    )(x_b16, w_main, *wg_cols, *wg_rows)


def _softcap(z):
    return GATE_SOFTCAP * jnp.tanh(z / GATE_SOFTCAP)


def _log_sigmoid(z):
    return jnp.minimum(z, 0.0) - jnp.log(1.0 + jnp.exp(-jnp.abs(z)))


def _mlstm_chunk_kernel(z_ref, gcol_ref, grow_ref, bcol_ref, brow_ref, g_ref, c0_ref, n0_ref, m0_ref,
                        h_ref, c_ref, n_ref, m_ref, c_sc, n_sc, m_sc, *, seq_len, tc):
    ci = pl.program_id(1)

    @pl.when(ci == 0)
    def _():
        c_sc[...] = c0_ref[0]
        n_sc[...] = n0_ref[0]
        m_sc[...] = m0_ref[0]

    t_col = ci * tc + lax.broadcasted_iota(jnp.int32, (tc, LANES), 0)
    lane = lax.broadcasted_iota(jnp.int32, (tc, LANES), 1)
    pre = _softcap(gcol_ref[0] + bcol_ref[...])
    gates_c = jnp.where(lane < HEADS, pre, _log_sigmoid(pre))
    neutral_c = jnp.where(lane < HEADS, NEG_BIG, 0.0)
    gates_c = jnp.where(t_col < seq_len, gates_c, neutral_c)
    t_row = ci * tc + lax.broadcasted_iota(jnp.int32, (2 * HEADS, tc), 1)
    sub = lax.broadcasted_iota(jnp.int32, (2 * HEADS, tc), 0)
    pre_r = _softcap(grow_ref[0] + brow_ref[...])
    gates_r = jnp.where(sub < HEADS, pre_r, _log_sigmoid(pre_r))
    neutral_r = jnp.where(sub < HEADS, NEG_BIG, 0.0)
    gates_r = jnp.where(t_row < seq_len, gates_r, neutral_r)

    ri = lax.broadcasted_iota(jnp.int32, (tc, tc), 0)
    cj = lax.broadcasted_iota(jnp.int32, (tc, tc), 1)
    causal = cj <= ri
    tril = jnp.where(causal, 1.0, 0.0).astype(bf16)
    triu = jnp.where(ri <= cj, 1.0, 0.0).astype(bf16)
    hi, lo = _split_hi_lo(gates_c)
    cum_c = jnp.dot(tril, hi, preferred_element_type=f32) + jnp.dot(tril, lo, preferred_element_type=f32)
    hi, lo = _split_hi_lo(gates_r)
    cum_r = jnp.dot(hi, triu, preferred_element_type=f32) + jnp.dot(lo, triu, preferred_element_type=f32)

    k_scale = M_DQK ** -0.5
    heads = range(HEADS)
    stack = lambda pieces: jnp.concatenate(pieces, axis=0)
    rows_of = lambda a, h: a[h * tc:(h + 1) * tc]
    q = [z_ref[0, :, h * M_DQK:(h + 1) * M_DQK] for h in heads]
    k = [z_ref[0, :, HEADS * M_DQK + h * M_DQK:HEADS * M_DQK + (h + 1) * M_DQK] for h in heads]
    v = [z_ref[0, :, D_MODEL + h * M_DV:D_MODEL + (h + 1) * M_DV] for h in heads]
    c0 = [c_sc[h] for h in heads]
    n0 = [n_sc[h:h + 1, :] for h in heads]
    m0 = [m_sc[h:h + 1, 0:1] for h in heads]
    b_col = stack([cum_c[:, HEADS + h:HEADS + h + 1] for h in heads])
    a_col = stack([gates_c[:, h:h + 1] for h in heads]) - b_col
    a_row = stack([jnp.broadcast_to(gates_r[h:h + 1, :] - cum_r[HEADS + h:HEADS + h + 1, :], (tc, tc))
                   for h in heads])
    m0_col = stack([jnp.broadcast_to(m0[h], (tc, 1)) for h in heads])

    assert tc & (tc - 1) == 0, "the stacked causal mask takes the token index as row & (tc - 1)"
    row_tok = jnp.bitwise_and(lax.broadcasted_iota(jnp.int32, (HEADS * tc, tc), 0), tc - 1)
    amat = jnp.where(lax.broadcasted_iota(jnp.int32, (HEADS * tc, tc), 1) <= row_tok, a_row, NEG_BIG)
    u = jnp.maximum(m0_col, jnp.max(amat, axis=-1, keepdims=True))
    dmat = jnp.exp(amat - u)
    g_in = jnp.exp(m0_col - u)
    s = stack([lax.dot_general(q[h], k[h], NT_DIMS, preferred_element_type=f32) for h in heads])
    w = s * dmat * k_scale
    w_b = w.astype(bf16)
    num = stack([jnp.dot(rows_of(w_b, h), v[h], preferred_element_type=f32) for h in heads])
    inter = stack([jnp.dot(q[h], c0[h].astype(bf16), preferred_element_type=f32) for h in heads])
    qn = stack([jnp.sum(q[h].astype(f32) * n0[h], axis=-1, keepdims=True) for h in heads])
    num = num + g_in * inter
    den = jnp.sum(w, axis=-1, keepdims=True) + g_in * qn
    hval = num / jnp.maximum(jnp.abs(den), jnp.exp(-(b_col + u)))
    hval = hval * lax.rsqrt(jnp.mean(hval * hval, axis=-1, keepdims=True) + RMS_EPS)
    og = stack([z_ref[0, :, 2 * D_MODEL + h * M_DV:2 * D_MODEL + (h + 1) * M_DV] for h in heads])
    g_rows = stack([jnp.broadcast_to(g_ref[:, h * M_DV:(h + 1) * M_DV], (tc, M_DV)) for h in heads])
    hval = (hval * g_rows * jax.nn.sigmoid(og.astype(f32))).astype(bf16)
    for h in heads:
        h_ref[0, :, h * M_DV:(h + 1) * M_DV] = rows_of(hval, h)

    u_last = [u[(h + 1) * tc - 1:(h + 1) * tc, :] for h in heads]
    decay = jnp.exp(a_col - stack([jnp.broadcast_to(u_last[h], (tc, 1)) for h in heads]))
    kd = stack([k[h].astype(f32) for h in heads]) * (decay * k_scale)
    kd_b = kd.astype(bf16)
    for h in heads:
        g0 = jnp.exp(m0[h] - u_last[h])
        c_sc[h] = g0 * c0[h] + lax.dot_general(rows_of(kd_b, h), v[h], TN_DIMS, preferred_element_type=f32)
        n_sc[h:h + 1, :] = g0 * n0[h] + jnp.sum(rows_of(kd, h), axis=0, keepdims=True)
        m_sc[h:h + 1, :] = jnp.broadcast_to(b_col[(h + 1) * tc - 1:(h + 1) * tc, :] + u_last[h], (1, LANES))

    @pl.when(ci == pl.num_programs(1) - 1)
    def _():
        c_ref[0] = c_sc[...]
        n_ref[0] = n_sc[...]
        m_ref[0] = m_sc[...]


def _mlstm_chunks(z, gcol, grow, bcol, brow, norm_g, c0t, n0, m0b, seq_len, tc):
    nb, lp, _ = z.shape
    nc = lp // tc
    per_b = lambda shape: pl.BlockSpec((1,) + shape, lambda b, c: (b,) + (0,) * len(shape))
    const = lambda shape: pl.BlockSpec(shape, lambda b, c: (0,) * len(shape))
    state_specs = (per_b((HEADS, M_DQK, M_DV)), per_b((HEADS, M_DQK)), per_b((HEADS, LANES)))
    return pl.pallas_call(
        functools.partial(_mlstm_chunk_kernel, seq_len=seq_len, tc=tc),
        out_shape=(jax.ShapeDtypeStruct((nb, lp, D_MODEL), bf16),
                   jax.ShapeDtypeStruct((nb, HEADS, M_DQK, M_DV), f32),
                   jax.ShapeDtypeStruct((nb, HEADS, M_DQK), f32),
                   jax.ShapeDtypeStruct((nb, HEADS, LANES), f32)),
        grid=(nb, nc),
        in_specs=[pl.BlockSpec((1, tc, 3 * D_MODEL), lambda b, c: (b, c, 0)),
                  pl.BlockSpec((1, tc, LANES), lambda b, c: (b, c, 0)),
                  pl.BlockSpec((1, 2 * HEADS, tc), lambda b, c: (b, 0, c)),
                  const((1, LANES)), const((2 * HEADS, 1)), const((1, D_MODEL))] + list(state_specs),
        out_specs=(pl.BlockSpec((1, tc, D_MODEL), lambda b, c: (b, c, 0)),) + state_specs,
        scratch_shapes=[pltpu.VMEM((HEADS, M_DQK, M_DV), f32), pltpu.VMEM((HEADS, M_DQK), f32),
                        pltpu.VMEM((HEADS, LANES), f32)],
        compiler_params=_params(("parallel", "arbitrary"), 32),
        name="mlstm_chunks",
    )(z, gcol, grow, bcol, brow, norm_g, c0t, n0, m0b)


ATT_TILE = 512
SAMPLE_TILE = 128
MLSTM_CHUNK = 128
MLSTM_SHORT_CHUNK = 16
MLSTM_PROJ_TILE = 512
PAGES_PER_STEP = 8
MOE_TILE = 1024
MOE_LN_TILE = 512
ROUTER_TILE = 512


def _attention_layer(xp, xp_b, xs, cache_k, cache_v, layer, page_table, w_in, lam_p, g, w_out, ln_g, ln_b, lam_init):
    nbd, n_new, _ = xs.shape
    seq_len = xp.shape[1]
    past = page_table.shape[1] * cache_k.shape[2]
    qb, kb, vb, kf, vf = _qkv_rope(xp_b, w_in.astype(bf16), jnp.arange(seq_len, dtype=jnp.int32), ATT_TILE)
    o = _flash_attention(qb, kb, vb, lam_p, g, lam_init, ATT_TILE)
    xp1, xp1_b = _proj_ln(o, xp, w_out.astype(bf16), ln_g, ln_b, ATT_TILE)
    pos_s = past + (jnp.arange(nbd * n_new, dtype=jnp.int32) % n_new)
    xs_rows = xs.reshape(1, nbd * n_new, D_MODEL)
    _, _, _, ksf, vsf, qs = _qkv_rope(xs_rows, w_in, pos_s, SAMPLE_TILE, q_f32=True)
    qs = qs.reshape(nbd, n_new, D_MODEL)
    ksf = ksf.reshape(nbd, n_new, D_MODEL)
    vsf = vsf.reshape(nbd, n_new, D_MODEL)
    n_pool, page = cache_k.shape[1], cache_k.shape[2]
    ck = cache_k.reshape(-1, page * HEADS, HEAD_W)
    cv = cache_v.reshape(-1, page * HEADS, HEAD_W)
    os_ = _decode_attention(qs, ksf, vsf, ck, cv, page_table, layer * n_pool, lam_p, g, lam_init, PAGES_PER_STEP)
    os_rows = os_.reshape(1, nbd * n_new, D_MODEL)
    xs1, xs1_b = _proj_ln(os_rows, xs_rows, w_out, ln_g, ln_b, SAMPLE_TILE)
    return (xp1, xp1_b, xs1.reshape(xs.shape), xs1_b.reshape(xs.shape),
            kf.reshape(kf.shape[:2] + (HEADS, HEAD_W)), vf.reshape(vf.shape[:2] + (HEADS, HEAD_W)),
            ksf.reshape(nbd, n_new, HEADS, HEAD_W), vsf.reshape(nbd, n_new, HEADS, HEAD_W))


def _mlstm_stream(x, x_b, w_main, wg_cols, wg_rows, bcol, brow, norm_g, c0t, n0, m0b, w_out, ln_g, ln_b):
    nb, seq_len, _ = x.shape
    short = seq_len < MLSTM_CHUNK
    chunk = MLSTM_SHORT_CHUNK if short else MLSTM_CHUNK
    if short:
        assert seq_len <= chunk
        rows = nb * seq_len
        z, gcol, grow = _mlstm_proj(x_b.reshape(1, rows, D_MODEL), w_main, wg_cols, wg_rows, rows, rows)
        pad = chunk - seq_len
        z = jnp.pad(z[0, :rows].reshape(nb, seq_len, 3 * D_MODEL), [(0, 0), (0, pad), (0, 0)])
        gcol = jnp.pad(gcol[0, :rows].reshape(nb, seq_len, LANES), [(0, 0), (0, pad), (0, 0)])
        grow = jnp.transpose(grow[0, :, :rows].reshape(2 * HEADS, nb, seq_len), (1, 0, 2))
        grow = jnp.pad(grow, [(0, 0), (0, 0), (0, pad)])
    else:
        z, gcol, grow = _mlstm_proj(x_b, w_main, wg_cols, wg_rows, MLSTM_PROJ_TILE, MLSTM_CHUNK)
    hb, ct, n, mb = _mlstm_chunks(z, gcol, grow, bcol, brow, norm_g, c0t, n0, m0b, seq_len, chunk)
    if short:
        hb = hb[:, :seq_len].reshape(1, rows, D_MODEL)
        x1, x1_b = _proj_ln(hb, x.reshape(1, rows, D_MODEL), w_out, ln_g, ln_b, rows)
        x1, x1_b = x1.reshape(x.shape), x1_b.reshape(x.shape)
    else:
        x1, x1_b = _proj_ln(hb, x, w_out, ln_g, ln_b, ATT_TILE)
    return x1, x1_b, jnp.swapaxes(ct, -1, -2), n, mb[..., 0]


def kernel(x_prompt, x_sample, cache_k, cache_v, state_C, state_n, state_m, page_table, meta_tokens, w_attn_in, lambda_q1, lambda_k1, lambda_q2, lambda_k2, subln_g, w_attn_out, w_mlstm_in, b_mlstm_if, mlstm_norm_g, w_mlstm_out, w_router, b_router, w_exp_gate, w_exp_up, w_exp_down, ln_g, ln_b):
    nb = x_prompt.shape[0]
    nbd = x_sample.shape[0]
    meta = jnp.broadcast_to(meta_tokens.astype(x_prompt.dtype)[None], (nb, N_META, D_MODEL))
    xp = jnp.concatenate([meta, x_prompt], axis=1)
    xs = x_sample
    xp_b = xp.astype(bf16)
    router_w = _split_hi_lo_glue(w_router.T) + (b_router.astype(f32).reshape(N_EXPERTS, 1),)
    prompt_moe_tile = min(MOE_TILE, xp.shape[0] * xp.shape[1])
    sample_rows = nbd * xs.shape[1]

    def moe_pair(i, xp, xs):
        args = (router_w, w_exp_gate[i], w_exp_up[i], w_exp_down[i], ln_g[i, 1:2], ln_b[i, 1:2])
        xp, xp_b = _moe_block(xp, *args, ROUTER_TILE, prompt_moe_tile, min(MOE_LN_TILE, prompt_moe_tile))
        xs, xs_b = _moe_block(xs, *args, sample_rows, sample_rows, sample_rows)
        return xp, xp_b, xs, xs_b

    lam_init = 0.8 - 0.6 * math.exp(-0.3 * 0)
    lam_p = jnp.stack([lambda_q1[0], lambda_k1[0], lambda_q2[0], lambda_k2[0]]).astype(f32)
    (xp, xp_b, xs, xs_b, k_p, v_p, k_s, v_s) = _attention_layer(
        xp, xp_b, xs, cache_k, cache_v, 0, page_table, w_attn_in[0], lam_p,
        subln_g[0].reshape(1, HEAD_W), w_attn_out[0], ln_g[0, 0:1], ln_b[0, 0:1], lam_init)
    xp, xp_b, xs, xs_b = moe_pair(0, xp, xs)

    w_in = w_mlstm_in[0]
    w_main = w_in[:, :3 * D_MODEL].astype(bf16)
    w_gate = w_in[:, 3 * D_MODEL:]
    wg_cols = _split_hi_lo_glue(jnp.pad(w_gate, [(0, 0), (0, LANES - 2 * HEADS)]))
    wg_rows = _split_hi_lo_glue(w_gate.T)
    b_if = b_mlstm_if[0].astype(f32).reshape(2 * HEADS)
    bcol = jnp.pad(b_if, (0, LANES - 2 * HEADS)).reshape(1, LANES)
    brow = b_if.reshape(2 * HEADS, 1)
    norm_g = mlstm_norm_g[0].reshape(1, D_MODEL)
    common = (w_main, wg_cols, wg_rows, bcol, brow, norm_g)
    tail = (w_mlstm_out[0].astype(bf16), ln_g[1, 0:1], ln_b[1, 0:1])
    zero_state = (jnp.zeros((nb, HEADS, M_DQK, M_DV), f32), jnp.zeros((nb, HEADS, M_DQK), f32),
                  jnp.zeros((nb, HEADS, LANES), f32))
    xp, xp_b, c_p, n_p, m_p = _mlstm_stream(xp, xp_b, *common, *zero_state, *tail)
    sample_state = (jnp.swapaxes(state_C[0].astype(f32), -1, -2), state_n[0].astype(f32),
                    jnp.broadcast_to(state_m[0].astype(f32)[..., None], (nbd, HEADS, LANES)))
    xs, xs_b, c_s, n_s, m_s = _mlstm_stream(xs, xs_b, *common, *sample_state, *tail)
    xp, xp_b, xs, xs_b = moe_pair(1, xp, xs)

    return (xp[:, N_META:], xs, k_p[None], v_p[None], k_s[None], v_s[None],
            c_p[None], n_p[None], m_p[None], c_s[None], n_s[None], m_s[None])
```
